```python
import jax, jax.numpy as jnp
from jax import lax
import numpy as np

D_MODEL = 1024
BATCH = 4
SEQ = 8192
DEPTH = 1
DEC_BATCH = 128
DEC_SEQ = 8
PAST_LEN = 8192
PAGE_SIZE = 128

RET_HEADS = 4
RET_DK = 128
RET_DV = 256
RET_CHUNK = 128
ROPE_THETA = 10000.0
NSA_HEADS = 8
NSA_KV_HEADS = 2
NSA_GROUP = NSA_HEADS // NSA_KV_HEADS
NSA_HEAD_DIM = 64
CMP_BLOCK = 32
CMP_STRIDE = 16
CMP_HIDDEN = 2 * NSA_HEAD_DIM
SEL_BLOCK = 64
SEL_TOP_N = 16
WINDOW = 512
NSA_QBLOCK = 64
FORCED_SCORE = 1.0e4
PEER_HEADS = 8
PEER_NKEYS = 128
PEER_N_EXPERTS = PEER_NKEYS * PEER_NKEYS
PEER_KEY_DIM = 256
PEER_HALF = PEER_KEY_DIM // 2
PEER_TOPK = 16
PEER_TOKEN_BLOCK = 128
DN_ALPHA = (2.0 * DEPTH) ** 0.25
DN_BETA = (8.0 * DEPTH) ** -0.25
LN_EPS = 1e-5
NEG_BIG = -1e30
RET_QK = RET_HEADS * RET_DK
RET_V = RET_HEADS * RET_DV
NSA_Q = NSA_HEADS * NSA_HEAD_DIM
NSA_KV = NSA_KV_HEADS * NSA_HEAD_DIM
IN_SECTIONS = (RET_QK, RET_QK, RET_V, RET_V, NSA_Q, 6 * NSA_KV, 3 * NSA_HEADS, 2 * D_MODEL)
IN_WIDTH = sum(IN_SECTIONS)
IN_SPLITS = tuple(int(s) for s in np.cumsum(IN_SECTIONS)[:-1])

kernel_name = "hybrid_retnet_nsa_peer_decoder_step"


def _normalize(x):
    xf = x.astype(jnp.float32)
    mu = jnp.mean(xf, -1, keepdims=True)
    var = jnp.mean(jnp.square(xf - mu), -1, keepdims=True)
    return (xf - mu) * lax.rsqrt(var + LN_EPS)


def layer_norm(x, g, b):
    return (_normalize(x) * g + b).astype(x.dtype)


def rope(x, pos):
    half = x.shape[-1] // 2
    inv = ROPE_THETA ** (-jnp.arange(half, dtype=jnp.float32) / half)
    ang = pos.astype(jnp.float32)[:, None] * inv[None, :]
    cos, sin = jnp.cos(ang)[:, None, :], jnp.sin(ang)[:, None, :]
    xf = x.astype(jnp.float32)
    x1, x2 = xf[..., :half], xf[..., half:]
    return jnp.concatenate([x1 * cos - x2 * sin, x1 * sin + x2 * cos], -1).astype(x.dtype)


def masked_softmax(s, mask):
    s = jnp.where(mask, s, NEG_BIG)
    p = jnp.where(mask, jnp.exp(s - jnp.max(s, -1, keepdims=True)), 0.0)
    return p / jnp.maximum(jnp.sum(p, -1, keepdims=True), 1.0)


def retention_log_decay():
    return jnp.log1p(-jnp.exp2(-5.0 - jnp.arange(RET_HEADS, dtype=jnp.float32)))


def retention_chunk(R, q, k, v, log_g):
    f32 = jnp.float32
    q, k, v = q.astype(f32), k.astype(f32), v.astype(f32)
    C = q.shape[2]
    n = jnp.arange(C, dtype=f32)
    lg = log_g[:, None, None]
    diff = n[:, None] - n[None, :]
    decay = jnp.where(diff >= 0, jnp.exp(lg * jnp.maximum(diff, 0.0)), 0.0)
    inner = jnp.einsum('bhnm,bhme->bhne', jnp.einsum('bhnd,bhmd->bhnm', q, k) * decay, v)
    cross = jnp.einsum('bhnd,bhde->bhne', q, R) * jnp.exp(lg * (n[:, None] + 1.0))
    k_dec = k * jnp.exp(lg * (C - 1.0 - n[:, None]))
    R_new = R * jnp.exp(log_g * C)[:, None, None] + jnp.einsum('bhmd,bhme->bhde', k_dec, v)
    return R_new, inner + cross


def retention_prompt(q, k, v):
    B, T, H, dk = q.shape
    C = min(RET_CHUNK, T)
    nC = T // C
    log_g = retention_log_decay()

    def to_chunks(a):
        return a.reshape(B, nC, C, H, a.shape[-1]).transpose(1, 0, 3, 2, 4)

    def step(R, qkv):
        return retention_chunk(R, qkv[0], qkv[1], qkv[2], log_g)

    R0 = jnp.zeros((B, H, dk, v.shape[-1]), jnp.float32)
    R, o = lax.scan(step, R0, (to_chunks(q), to_chunks(k), to_chunks(v)))
    return o.transpose(1, 0, 3, 2, 4).reshape(B, T, H, -1), R


def compress_halves(rows, w1):
    B, L = rows.shape[:2]
    hv = rows.reshape(B, L // CMP_STRIDE, CMP_STRIDE, NSA_KV_HEADS, NSA_HEAD_DIM)
    w = w1.reshape(2, CMP_STRIDE, NSA_HEAD_DIM, CMP_HIDDEN)
    return jnp.einsum('bnskd,psde->bnkpe', hv, w)


def compress_blocks(z, pe, w1, w2):
    bias = pe.reshape(-1) @ w1
    hid = jax.nn.gelu(z[:, :-1, :, 0] + z[:, 1:, :, 1] + bias, approximate=False)
    return hid @ w2


def nsa_attend(q, qpos, gates, kc, vc, gather_sel, n_sel, kw, vw, kwpos):
    f32 = jnp.float32
    B, T = q.shape[:2]
    qf = q.astype(f32) * NSA_HEAD_DIM ** -0.5
    n_cmp = kc.shape[1]
    c_start = jnp.arange(n_cmp) * CMP_STRIDE
    c_ok = (c_start + CMP_BLOCK - 1)[None, :] <= qpos[:, None]
    p_c = masked_softmax(jnp.einsum('btkgd,bnkd->bkgtn', qf, kc.astype(f32)), c_ok)
    o_c = jnp.einsum('bkgtn,bnkd->btkgd', p_c, vc.astype(f32))
    j = jnp.arange(n_sel)
    overlap = ((c_start[:, None] < (j[None, :] + 1) * SEL_BLOCK)
               & (c_start[:, None] + CMP_BLOCK > j[None, :] * SEL_BLOCK)).astype(f32)
    imp = jnp.einsum('bkgtn,ns->bkts', p_c, overlap)
    cur = (qpos // SEL_BLOCK)[:, None]
    forced = (j == 0) | (j == cur) | (j == cur - 1)
    score = jnp.where(j <= cur, jnp.where(forced, FORCED_SCORE, imp), -1.0)
    top_score, idx = lax.top_k(score, min(SEL_TOP_N, n_sel))
    ks, vs = gather_sel(idx)
    m = idx.shape[-1] * SEL_BLOCK
    tok = idx[..., None] * SEL_BLOCK + jnp.arange(SEL_BLOCK)
    s_ok = (top_score >= 0)[..., None] & (tok <= qpos[:, None, None])
    ks = ks.reshape(B, NSA_KV_HEADS, T, m, NSA_HEAD_DIM).astype(f32)
    vs = vs.reshape(B, NSA_KV_HEADS, T, m, NSA_HEAD_DIM).astype(f32)
    p_s = masked_softmax(jnp.einsum('btkgd,bktmd->bkgtm', qf, ks), s_ok.reshape(B, NSA_KV_HEADS, 1, T, m))
    o_s = jnp.einsum('bkgtm,bktmd->btkgd', p_s, vs)
    w_ok = ((kwpos[None, :] <= qpos[:, None]) & (kwpos[None, :] > qpos[:, None] - WINDOW)
            & (kwpos[None, :] >= 0))
    p_w = masked_softmax(jnp.einsum('btkgd,bskd->bkgts', qf, kw.astype(f32)), w_ok)
    o_w = jnp.einsum('bkgts,bskd->btkgd', p_w, vw.astype(f32))
    g = gates.astype(f32)
    return g[..., 0:1] * o_c + g[..., 1:2] * o_s + g[..., 2:3] * o_w


def mixer_inputs(h, pos, w_in):
    B, T, _ = h.shape
    rq, rk, rv, rg, nq, nkv, ng, mg = jnp.split(h @ w_in, IN_SPLITS, axis=-1)
    rq = rope(rq.reshape(B, T, RET_HEADS, RET_DK), pos)
    rk = rope(rk.reshape(B, T, RET_HEADS, RET_DK), pos) * RET_DK ** -0.5
    rv = rv.reshape(B, T, RET_HEADS, RET_DV)
    nq = rope(nq.reshape(B, T, NSA_HEADS, NSA_HEAD_DIM), pos).reshape(B, T, NSA_KV_HEADS, NSA_GROUP, NSA_HEAD_DIM)
    nkv = nkv.reshape(B, T, 6, NSA_KV_HEADS, NSA_HEAD_DIM)
    kc, vc = rope(nkv[:, :, 0], pos), nkv[:, :, 1]
    ks, vs = rope(nkv[:, :, 2], pos), nkv[:, :, 3]
    kw, vw = rope(nkv[:, :, 4], pos), nkv[:, :, 5]
    ng = jax.nn.sigmoid(ng.reshape(B, T, NSA_KV_HEADS, NSA_GROUP, 3).astype(jnp.float32))
    mg = jax.nn.sigmoid(mg.reshape(B, T, 2, D_MODEL).astype(jnp.float32)).astype(h.dtype)
    return rq, rk, rv, rg, nq, kc, vc, ks, vs, kw, vw, ng, mg


def mixer_merge(o_r, rg, o_n, mg, w_br_ret, w_br_nsa, w_out):
    B, T = rg.shape[:2]
    dt = rg.dtype
    a = (_normalize(o_r).reshape(B, T, RET_V) * jax.nn.silu(rg.astype(jnp.float32))).astype(dt) @ w_br_ret
    n = o_n.reshape(B, T, NSA_Q).astype(dt) @ w_br_nsa
    return (mg[:, :, 0] * a + mg[:, :, 1] * n) @ w_out


def token_mixer_prompt(h, mw):
    w_in, pe_k, w1_k, w2_k, pe_v, w1_v, w2_v, w_br_ret, w_br_nsa, w_out = mw
    B, T, _ = h.shape
    pos = jnp.arange(T, dtype=jnp.int32)
    rq, rk, rv, rg, nq, kc, vc, ks, vs, kw, vw, ng, mg = mixer_inputs(h, pos, w_in)
    o_r, R = retention_prompt(rq, rk, rv)
    kcb = compress_blocks(compress_halves(kc, w1_k), pe_k, w1_k, w2_k)
    vcb = compress_blocks(compress_halves(vc, w1_v), pe_v, w1_v, w2_v)
    n_sel = T // SEL_BLOCK
    ksb = ks.reshape(B, n_sel, SEL_BLOCK, NSA_KV_HEADS, NSA_HEAD_DIM)
    vsb = vs.reshape(B, n_sel, SEL_BLOCK, NSA_KV_HEADS, NSA_HEAD_DIM)
    bi = jnp.arange(B)[:, None, None, None]
    hi = jnp.arange(NSA_KV_HEADS)[None, :, None, None]

    def gather_sel(idx):
        return ksb[bi, idx, :, hi, :], vsb[bi, idx, :, hi, :]

    pad = ((0, 0), (WINDOW, 0), (0, 0), (0, 0))
    kwp, vwp = jnp.pad(kw, pad), jnp.pad(vw, pad)
    QB = min(NSA_QBLOCK, T)

    def block(i):
        st = i * QB
        qpos = st + jnp.arange(QB, dtype=jnp.int32)
        kwpos = st - WINDOW + jnp.arange(WINDOW + QB, dtype=jnp.int32)
        return nsa_attend(lax.dynamic_slice_in_dim(nq, st, QB, 1), qpos,
                          lax.dynamic_slice_in_dim(ng, st, QB, 1), kcb, vcb, gather_sel, n_sel,
                          lax.dynamic_slice_in_dim(kwp, st, WINDOW + QB, 1),
                          lax.dynamic_slice_in_dim(vwp, st, WINDOW + QB, 1), kwpos)

    o_n = jnp.moveaxis(lax.map(block, jnp.arange(T // QB)), 0, 1).reshape(B, T, NSA_KV_HEADS, NSA_GROUP, NSA_HEAD_DIM)
    y = mixer_merge(o_r, rg, o_n, mg, w_br_ret, w_br_nsa, w_out)
    wb = min(WINDOW, T)
    return y, (R.astype(h.dtype), kc, vc, ks, vs, kw[:, T - wb:], vw[:, T - wb:])


def token_mixer_sample(h, mw, state_ret, cache_cmp_k, cache_cmp_v, cache_sel_k, cache_sel_v,
                       cache_win_k, cache_win_v, page_table):
    w_in, pe_k, w1_k, w2_k, pe_v, w1_v, w2_v, w_br_ret, w_br_nsa, w_out = mw
    B, T, _ = h.shape
    n_pages = page_table.shape[1]
    past = n_pages * PAGE_SIZE
    pos = past + jnp.arange(T, dtype=jnp.int32)
    rq, rk, rv, rg, nq, kc, vc, ks, vs, kw, vw, ng, mg = mixer_inputs(h, pos, w_in)
    r_new, o_r = retention_chunk(state_ret.astype(jnp.float32), jnp.swapaxes(rq, 1, 2), jnp.swapaxes(rk, 1, 2),
                                 jnp.swapaxes(rv, 1, 2), retention_log_decay())
    o_r = jnp.swapaxes(o_r, 1, 2)
    n_new16 = (T // CMP_STRIDE) * CMP_STRIDE

    def summaries(pool, new, pe, w1, w2):
        rows = pool[page_table].reshape(B, past, NSA_KV_HEADS, NSA_HEAD_DIM)
        z = jnp.concatenate([compress_halves(rows, w1), compress_halves(new[:, :n_new16], w1)], axis=1)
        return compress_blocks(z, pe, w1, w2)

    kcb = summaries(cache_cmp_k, kc, pe_k, w1_k, w2_k)
    vcb = summaries(cache_cmp_v, vc, pe_v, w1_v, w2_v)
    n_sel = -(-(past + T) // SEL_BLOCK)
    past_blocks = past // SEL_BLOCK
    per_page = PAGE_SIZE // SEL_BLOCK
    n_new_blocks = -(-T // SEL_BLOCK)
    bi = jnp.arange(B)[:, None, None, None]
    hi = jnp.arange(NSA_KV_HEADS)[None, :, None, None]

    def new_blocks(a):
        a = jnp.pad(a, ((0, 0), (0, n_new_blocks * SEL_BLOCK - T), (0, 0), (0, 0)))
        return a.reshape(B, n_new_blocks, SEL_BLOCK, NSA_KV_HEADS, NSA_HEAD_DIM)

    sel_src = ((cache_sel_k.reshape(-1, SEL_BLOCK, NSA_KV_HEADS, NSA_HEAD_DIM), new_blocks(ks)),
               (cache_sel_v.reshape(-1, SEL_BLOCK, NSA_KV_HEADS, NSA_HEAD_DIM), new_blocks(vs)))

    def gather_sel(idx):
        page = jnp.minimum(idx // per_page, n_pages - 1)
        phys = page_table[bi, page] * per_page + idx % per_page
        jn = jnp.clip(idx - past_blocks, 0, n_new_blocks - 1)
        in_past = (idx < past_blocks)[..., None, None]
        return tuple(jnp.where(in_past, pool[phys, :, hi, :], new[bi, jn, :, hi, :]) for pool, new in sel_src)

    wb = cache_win_k.shape[1]
    kwin = jnp.concatenate([cache_win_k, kw], axis=1)
    vwin = jnp.concatenate([cache_win_v, vw], axis=1)
    kwpos = past - wb + jnp.arange(wb + T, dtype=jnp.int32)

    def one_query(args):
        qt, gt, pt = args
        return nsa_attend(qt[:, None], pt[None], gt[:, None], kcb, vcb, gather_sel, n_sel, kwin, vwin, kwpos)[:, 0]

    o_n = jnp.swapaxes(lax.map(one_query, (jnp.swapaxes(nq, 0, 1), jnp.swapaxes(ng, 0, 1), pos)), 0, 1)
    y = mixer_merge(o_r, rg, o_n, mg, w_br_ret, w_br_nsa, w_out)
    return y, (r_new.astype(h.dtype), kc, vc, ks, vs, kwin[:, T:], vwin[:, T:])


def peer_ffn(h, wq, subkeys, u_tab, v_tab):
    shape = h.shape
    xf = h.reshape(-1, D_MODEL)
    n = xf.shape[0]
    nb = -(-n // PEER_TOKEN_BLOCK)
    xp = jnp.pad(xf, ((0, nb * PEER_TOKEN_BLOCK - n), (0, 0))).reshape(nb, PEER_TOKEN_BLOCK, D_MODEL)

    def block(xb):
        q = (xb @ wq).reshape(-1, PEER_HEADS, 2, PEER_HALF).astype(jnp.float32)
        s = jnp.einsum('nhpd,hpkd->nhpk', q, subkeys.astype(jnp.float32))
        sv, si = lax.top_k(s, PEER_TOPK)
        cand = (sv[:, :, 0, :, None] + sv[:, :, 1, None, :]).reshape(-1, PEER_HEADS, PEER_TOPK * PEER_TOPK)
        cidx = (si[:, :, 0, :, None] * PEER_NKEYS + si[:, :, 1, None, :]).reshape(-1, PEER_HEADS, PEER_TOPK * PEER_TOPK)
        top, sel = lax.top_k(cand, PEER_TOPK)
        expert = jnp.take_along_axis(cidx, sel, axis=-1)
        g = jax.nn.softmax(top, axis=-1)
        act = jax.nn.gelu(jnp.einsum('nd,nhkd->nhk', xb, u_tab[expert]).astype(jnp.float32), approximate=False)
        return jnp.einsum('nhk,nhkd->nd', (g * act).astype(xb.dtype), v_tab[expert])

    return lax.map(block, xp).reshape(-1, D_MODEL)[:n].reshape(shape)


def decoder_layer(x, c, mixer, w_ada, b_ada, ln1_g, ln1_b, ln2_g, ln2_b, peer_wq, peer_subkeys, peer_u, peer_v):
    sh1, sc1, g1, sh2, sc2, g2 = jnp.split((c @ w_ada + b_ada)[:, None, :], 6, axis=-1)
    y, state = mixer(x * (1 + sc1) + sh1)
    x = layer_norm(DN_ALPHA * x + g1 * y, ln1_g, ln1_b)
    f = peer_ffn(x * (1 + sc2) + sh2, peer_wq, peer_subkeys, peer_u, peer_v)
    x = layer_norm(DN_ALPHA * x + g2 * f, ln2_g, ln2_b)
    return x, state


def setup_inputs(seed: int = 0) -> dict:
    key = jax.random.key(seed)
    ks = iter(jax.random.split(key, 40))
    f32 = jnp.float32
    L = DEPTH
    n_pages = PAST_LEN // PAGE_SIZE
    n_pool = (DEC_BATCH * n_pages * 5) // 4
    wb = min(WINDOW, PAST_LEN)

    def nrm(shape, scale):
        return jax.random.normal(next(ks), shape, f32) * scale

    pool_shape = (L, n_pool, PAGE_SIZE, NSA_KV_HEADS, NSA_HEAD_DIM)
    win_shape = (L, DEC_BATCH, wb, NSA_KV_HEADS, NSA_HEAD_DIM)
    out = {}
    out['x_prompt'] = nrm((BATCH, SEQ, D_MODEL), 1.0)
    out['x_sample'] = nrm((DEC_BATCH, DEC_SEQ, D_MODEL), 1.0)
    out['c_prompt'] = nrm((BATCH, D_MODEL), 1.0)
    out['c_sample'] = nrm((DEC_BATCH, D_MODEL), 1.0)
    out['state_ret'] = nrm((L, DEC_BATCH, RET_HEADS, RET_DK, RET_DV), 0.5)
    out['cache_cmp_k'] = nrm(pool_shape, 1.0)
    out['cache_cmp_v'] = nrm(pool_shape, 1.0)
    out['cache_sel_k'] = nrm(pool_shape, 1.0)
    out['cache_sel_v'] = nrm(pool_shape, 1.0)
    out['cache_win_k'] = nrm(win_shape, 1.0)
    out['cache_win_v'] = nrm(win_shape, 1.0)
    perm = jax.random.permutation(next(ks), n_pool)
    out['page_table'] = perm[:DEC_BATCH * n_pages].reshape(DEC_BATCH, n_pages).astype(jnp.int32)
    out['w_ada'] = nrm((L, D_MODEL, 6 * D_MODEL), 0.5 * D_MODEL ** -0.5)
    out['b_ada'] = nrm((L, 6 * D_MODEL), 0.01)
    out['w_in'] = nrm((L, D_MODEL, IN_WIDTH), D_MODEL ** -0.5)
    cin = CMP_BLOCK * NSA_HEAD_DIM
    out['cmp_pe_k'] = nrm((L, CMP_BLOCK, NSA_HEAD_DIM), 0.5)
    out['cmp_w1_k'] = nrm((L, cin, CMP_HIDDEN), cin ** -0.5)
    out['cmp_w2_k'] = nrm((L, CMP_HIDDEN, NSA_HEAD_DIM), CMP_HIDDEN ** -0.5)
    out['cmp_pe_v'] = nrm((L, CMP_BLOCK, NSA_HEAD_DIM), 0.5)
    out['cmp_w1_v'] = nrm((L, cin, CMP_HIDDEN), cin ** -0.5)
    out['cmp_w2_v'] = nrm((L, CMP_HIDDEN, NSA_HEAD_DIM), CMP_HIDDEN ** -0.5)
    out['w_br_ret'] = nrm((L, RET_V, D_MODEL), DN_BETA * RET_V ** -0.5)
    out['w_br_nsa'] = nrm((L, NSA_Q, D_MODEL), DN_BETA * NSA_Q ** -0.5)
    out['w_out'] = nrm((L, D_MODEL, D_MODEL), DN_BETA * D_MODEL ** -0.5)
    out['ln1_g'] = 1.0 + nrm((L, D_MODEL), 0.02)
    out['ln1_b'] = nrm((L, D_MODEL), 0.02)
    out['peer_wq'] = nrm((L, D_MODEL, PEER_HEADS * PEER_KEY_DIM), D_MODEL ** -0.5)
    out['peer_subkeys'] = nrm((L, PEER_HEADS, 2, PEER_NKEYS, PEER_HALF), PEER_HALF ** -0.5)
    out['peer_u'] = nrm((L, PEER_N_EXPERTS, D_MODEL), D_MODEL ** -0.5)
    out['peer_v'] = nrm((L, PEER_N_EXPERTS, D_MODEL), DN_BETA)
    out['ln2_g'] = 1.0 + nrm((L, D_MODEL), 0.02)
    out['ln2_b'] = nrm((L, D_MODEL), 0.02)
    return out


def reference(x_prompt, x_sample, c_prompt, c_sample, state_ret, cache_cmp_k, cache_cmp_v, cache_sel_k,
              cache_sel_v, cache_win_k, cache_win_v, page_table, w_ada, b_ada, w_in, cmp_pe_k, cmp_w1_k,
              cmp_w2_k, cmp_pe_v, cmp_w1_v, cmp_w2_v, w_br_ret, w_br_nsa, w_out, ln1_g, ln1_b, peer_wq,
              peer_subkeys, peer_u, peer_v, ln2_g, ln2_b):
    xp, xs = x_prompt, x_sample
    prompt_states, sample_states = [], []
    for l in range(DEPTH):
        mw = (w_in[l], cmp_pe_k[l], cmp_w1_k[l], cmp_w2_k[l], cmp_pe_v[l], cmp_w1_v[l], cmp_w2_v[l],
              w_br_ret[l], w_br_nsa[l], w_out[l])
        shared = (w_ada[l], b_ada[l], ln1_g[l], ln1_b[l], ln2_g[l], ln2_b[l],
                  peer_wq[l], peer_subkeys[l], peer_u[l], peer_v[l])
        xp, sp = decoder_layer(xp, c_prompt, lambda h: token_mixer_prompt(h, mw), *shared)
        xs, ss = decoder_layer(
            xs, c_sample,
            lambda h: token_mixer_sample(h, mw, state_ret[l], cache_cmp_k[l], cache_cmp_v[l], cache_sel_k[l],
                                         cache_sel_v[l], cache_win_k[l], cache_win_v[l], page_table),
            *shared)
        prompt_states.append(sp)
        sample_states.append(ss)
    ret_p, cmpk_p, cmpv_p, selk_p, selv_p, wink_p, winv_p = [jnp.stack(a) for a in zip(*prompt_states)]
    ret_s, cmpk_s, cmpv_s, selk_s, selv_s, wink_s, winv_s = [jnp.stack(a) for a in zip(*sample_states)]
    return (xp, xs, ret_p, cmpk_p, cmpv_p, selk_p, selv_p, wink_p, winv_p,
            ret_s, cmpk_s, cmpv_s, selk_s, selv_s, wink_s, winv_s)
```

```python
import functools
import math

import jax
import jax.numpy as jnp
import numpy as np
from jax import lax
from jax.experimental import pallas as pl
from jax.experimental.pallas import tpu as pltpu

_F32 = jnp.float32
_BF16 = jnp.bfloat16

_D = 1024
_RET_H, _RET_DK, _RET_DV, _RET_C = 4, 128, 256, 128
_NSA_H, _NSA_KVH, _NSA_G, _NSA_DH = 8, 2, 4, 64
_CMP_BLOCK, _CMP_STRIDE, _CMP_HID = 32, 16, 128
_SEL_BLOCK, _SEL_TOPN, _WINDOW, _QB = 64, 16, 512, 64
_FORCED = 1.0e4
_PEER_H, _PEER_NK, _PEER_HALF, _PEER_TOPK = 8, 128, 128, 16
_PAGE = 128
_ALPHA = 2.0 ** 0.25
_LN_EPS = 1e-5
_NEG = -1e30
_ROPE_THETA = 10000.0
_RET_QK = _RET_H * _RET_DK
_RET_V = _RET_H * _RET_DV
_NSA_Q = _NSA_H * _NSA_DH
_NSA_KV = _NSA_KVH * _NSA_DH
_IN_SECTIONS = (_RET_QK, _RET_QK, _RET_V, _RET_V, _NSA_Q, 6 * _NSA_KV, 3 * _NSA_H, 2 * _D)
_IN_SPLITS = tuple(int(s) for s in np.cumsum(_IN_SECTIONS)[:-1])

_VMEM_LIMIT = 48 * 1024 * 1024

_TN = 512
_C_RQ, _C_RK, _C_RV, _C_RG, _C_MG, _C_NQ = 0, 512, 1024, 2048, 3072, 5120
_C_KC, _C_KS, _C_KW = 5632, 5760, 5888
_C_VC, _C_VS, _C_VW, _C_NG = 6144, 6272, 6400, 6528
_IN_COLS = 13 * _TN


def _cparams(sem):
    return pltpu.CompilerParams(dimension_semantics=sem, vmem_limit_bytes=_VMEM_LIMIT)


def _gelu(x):
    return 0.5 * x * (1.0 + lax.erf(x * (1.0 / math.sqrt(2.0))))


def _layer_norm(r, g, b):
    mu = jnp.mean(r, -1, keepdims=True)
    d = r - mu
    var = jnp.mean(d * d, -1, keepdims=True)
    return d * lax.rsqrt(var + _LN_EPS) * g + b


def _masked_softmax(s, mask):
    s = jnp.where(mask, s, _NEG)
    p = jnp.where(mask, jnp.exp(s - jnp.max(s, -1, keepdims=True)), 0.0)
    return p / jnp.maximum(jnp.sum(p, -1, keepdims=True), 1.0)


def _linear_body(a_ref, w_ref, b_ref, o_ref):
    o_ref[...] = jnp.dot(a_ref[...].astype(_BF16), w_ref[...], preferred_element_type=_F32) + b_ref[...]


def _linear(a, w, b, tn=512):
    m, k = a.shape
    n = w.shape[1]
    return pl.pallas_call(
        _linear_body,
        grid=(n // tn,),
        in_specs=[pl.BlockSpec((m, k), lambda j: (0, 0)),
                  pl.BlockSpec((k, tn), lambda j: (0, j)),
                  pl.BlockSpec((1, tn), lambda j: (0, j))],
        out_specs=pl.BlockSpec((m, tn), lambda j: (0, j)),
        out_shape=jax.ShapeDtypeStruct((m, n), _F32),
        compiler_params=_cparams(("arbitrary",)),
        name="ada_linear",
    )(a, w.astype(_BF16), b.reshape(1, n))


def _build_wcat(w_in):
    rq, rk, rv, rg, nq, nkv, ng, mg = jnp.split(w_in, _IN_SPLITS, axis=1)
    nkv6 = nkv.reshape(_D, 6, _NSA_KV)
    kc, vc, ks, vs, kw, vw = [nkv6[:, i] for i in range(6)]
    z = jnp.zeros((_D, _NSA_KV), w_in.dtype)
    ngp = jnp.pad(ng, ((0, 0), (0, _NSA_KV - ng.shape[1])))
    wcat = jnp.concatenate([rq, rk, rv, rg, mg, nq, kc, ks, kw, z, vc, vs, vw, ngp], axis=1)
    return wcat.astype(_BF16)


def _rope_tables(pos):
    out = []
    for half in (_RET_DK // 2, _NSA_DH // 2):
        inv = _ROPE_THETA ** (-jnp.arange(half, dtype=_F32) / half)
        ang = pos.astype(_F32)[:, None] * inv[None, :]
        cos, sin = jnp.cos(ang), jnp.sin(ang)
        reps = _TN // (2 * half)
        out.append(jnp.tile(jnp.concatenate([cos, cos], -1), (1, reps)))
        out.append(jnp.tile(jnp.concatenate([-sin, sin], -1), (1, reps)))
    return out


def _inproj_body(x_ref, sc_ref, sh_ref, w_ref, c128_ref, s128_ref, c64_ref, s64_ref, y_ref, a_scr):
    j = pl.program_id(1)

    @pl.when(j == 0)
    def _():
        a_scr[...] = (x_ref[...] * (1.0 + sc_ref[0]) + sh_ref[0]).astype(_BF16)

    y = jnp.dot(a_scr[...], w_ref[...], preferred_element_type=_F32)
    width = y.shape[1]
    lane = lax.broadcasted_iota(jnp.int32, y.shape, 1)

    def rope(half, c_ref, s_ref):
        first = (lane % (2 * half)) < half
        partner = jnp.where(first, pltpu.roll(y, width - half, 1), pltpu.roll(y, half, 1))
        return y * c_ref[...] + partner * s_ref[...]

    @pl.when(j == _C_RQ // _TN)
    def _():
        y_ref[...] = rope(_RET_DK // 2, c128_ref, s128_ref)

    @pl.when(j == _C_RK // _TN)
    def _():
        y_ref[...] = rope(_RET_DK // 2, c128_ref, s128_ref) * (_RET_DK ** -0.5)

    @pl.when((j >= _C_RV // _TN) & (j < _C_RG // _TN))
    def _():
        y_ref[...] = y

    @pl.when((j >= _C_RG // _TN) & (j < _C_MG // _TN))
    def _():
        y_ref[...] = y * jax.nn.sigmoid(y)

    @pl.when((j >= _C_MG // _TN) & (j < _C_NQ // _TN))
    def _():
        y_ref[...] = jax.nn.sigmoid(y)

    @pl.when((j >= _C_NQ // _TN) & (j < _C_VC // _TN))
    def _():
        y_ref[...] = rope(_NSA_DH // 2, c64_ref, s64_ref)

    @pl.when(j == _C_VC // _TN)
    def _():
        y_ref[...] = jnp.where(lane < _C_NG - _C_VC, y, jax.nn.sigmoid(y))


def _inproj(x2, sc3, sh3, mod_map, wcat, tabs, tab_map, tm):
    n = x2.shape[0]
    r = sc3.shape[1]
    tab_spec = pl.BlockSpec((tm, _TN), lambda i, j: (tab_map(i), 0))
    return pl.pallas_call(
        _inproj_body,
        grid=(n // tm, _IN_COLS // _TN),
        in_specs=[pl.BlockSpec((tm, _D), lambda i, j: (i, 0)),
                  pl.BlockSpec((1, r, _D), lambda i, j: (mod_map(i), 0, 0)),
                  pl.BlockSpec((1, r, _D), lambda i, j: (mod_map(i), 0, 0)),
                  pl.BlockSpec((_D, _TN), lambda i, j: (0, j)),
                  tab_spec, tab_spec, tab_spec, tab_spec],
        out_specs=pl.BlockSpec((tm, _TN), lambda i, j: (i, j)),
        out_shape=jax.ShapeDtypeStruct((n, _IN_COLS), _F32),
        scratch_shapes=[pltpu.VMEM((tm, _D), _BF16)],
        compiler_params=_cparams(("parallel", "arbitrary")),
        name="in_proj",
    )(x2, sc3, sh3, wcat, *tabs)


def _merge_body(o_ref, srg_ref, on_ref, mg0_ref, mg1_ref, x_ref, g1_ref, wbr_ref, wbn_ref, wout_ref,
                lng_ref, lnb_ref, out_ref):
    o = o_ref[...]
    parts = []
    for h in range(_RET_H):
        seg = o[:, h * _RET_DV:(h + 1) * _RET_DV]
        mu = jnp.mean(seg, -1, keepdims=True)
        d = seg - mu
        var = jnp.mean(d * d, -1, keepdims=True)
        parts.append(d * lax.rsqrt(var + _LN_EPS))
    a = (jnp.concatenate(parts, axis=1) * srg_ref[...]).astype(_BF16)
    br = jnp.dot(a, wbr_ref[...], preferred_element_type=_F32)
    bn = jnp.dot(on_ref[...].astype(_BF16), wbn_ref[...], preferred_element_type=_F32)
    m = (mg0_ref[...] * br + mg1_ref[...] * bn).astype(_BF16)
    yv = jnp.dot(m, wout_ref[...], preferred_element_type=_F32)
    r = _ALPHA * x_ref[...] + g1_ref[0] * yv
    out_ref[...] = _layer_norm(r, lng_ref[...], lnb_ref[...])


def _merge(o_r, y, o_n, x2, g13, mod_map, w_br_ret, w_br_nsa, w_out, ln_g, ln_b, tm):
    n = x2.shape[0]
    r = g13.shape[1]
    row = lambda c: pl.BlockSpec((tm, _D), lambda i: (i, c))
    full = lambda a: pl.BlockSpec(a.shape, lambda i: (0,) * a.ndim)
    wbr, wbn, wout = w_br_ret.astype(_BF16), w_br_nsa.astype(_BF16), w_out.astype(_BF16)
    lng, lnb = ln_g.reshape(1, _D), ln_b.reshape(1, _D)
    return pl.pallas_call(
        _merge_body,
        grid=(n // tm,),
        in_specs=[row(0), row(_C_RG // _D), pl.BlockSpec((tm, _NSA_Q), lambda i: (i, 0)),
                  row(_C_MG // _D), row(_C_MG // _D + 1), row(0),
                  pl.BlockSpec((1, r, _D), lambda i: (mod_map(i), 0, 0)),
                  full(wbr), full(wbn), full(wout), full(lng), full(lnb)],
        out_specs=row(0),
        out_shape=jax.ShapeDtypeStruct((n, _D), _F32),
        compiler_params=_cparams(("parallel",)),
        name="merge_ln1",
    )(o_r, y, o_n, y, y, x2, g13, wbr, wbn, wout, lng, lnb)


def _nsa_prompt_body(q_ref, g_ref, kc_ref, vc_ref, ks_ref, vs_ref, kw_ref, vw_ref, o_ref, *, seq):
    qb = pl.program_id(2)
    rows = _NSA_G * _QB
    q = q_ref[0, 0, 0]
    row = lax.broadcasted_iota(jnp.int32, (rows, 1), 0)
    tpos = qb * _QB + row % _QB
    nt = (((1,), (1,)), ((), ()))

    n_cmp = kc_ref.shape[2]
    s_c = lax.dot_general(q, kc_ref[0, 0], nt, preferred_element_type=_F32)
    n_i = lax.broadcasted_iota(jnp.int32, (1, n_cmp), 1)
    p_c = _masked_softmax(s_c, (n_i * _CMP_STRIDE + _CMP_BLOCK - 1) <= tpos)
    o_c = jnp.dot(p_c.astype(_BF16), vc_ref[0, 0], preferred_element_type=_F32)

    pcs = p_c[0:_QB] + p_c[_QB:2 * _QB] + p_c[2 * _QB:3 * _QB] + p_c[3 * _QB:4 * _QB]
    n_sel_pad = 128
    nn = lax.broadcasted_iota(jnp.int32, (n_cmp, n_sel_pad), 0)
    jj = lax.broadcasted_iota(jnp.int32, (n_cmp, n_sel_pad), 1)
    per = _SEL_BLOCK // _CMP_STRIDE
    ov = ((nn < (jj + 1) * per) & (nn * _CMP_STRIDE + _CMP_BLOCK > jj * _SEL_BLOCK)).astype(_BF16)
    hi = pcs.astype(_BF16)
    lo = (pcs - hi.astype(_F32)).astype(_BF16)
    imp = jnp.dot(hi, ov, preferred_element_type=_F32) + jnp.dot(lo, ov, preferred_element_type=_F32)
    j_i = lax.broadcasted_iota(jnp.int32, (_QB, n_sel_pad), 1)
    forced = (j_i == 0) | (j_i == qb) | (j_i == qb - 1)
    score = jnp.where(j_i <= qb, jnp.where(forced, _FORCED, imp), -1.0)
    work = score
    sel = jnp.zeros(score.shape, jnp.bool_)
    for _ in range(_SEL_TOPN):
        mx = jnp.max(work, -1, keepdims=True)
        idx = jnp.min(jnp.where(work == mx, j_i, n_sel_pad), -1, keepdims=True)
        pick = j_i == idx
        sel = sel | pick
        work = jnp.where(pick, -jnp.inf, work)
    selb = (sel & (score >= 0.0)).astype(_BF16)

    chunk = ks_ref.shape[3]
    bpc = chunk // _SEL_BLOCK
    e_j = lax.broadcasted_iota(jnp.int32, (n_sel_pad, chunk), 0)
    e_k = lax.broadcasted_iota(jnp.int32, (n_sel_pad, chunk), 1) // _SEL_BLOCK
    k_i = lax.broadcasted_iota(jnp.int32, (1, chunk), 1)

    def chunk_step(c, carry):
        m, l, acc = carry
        s = lax.dot_general(q, ks_ref[0, 0, c], nt, preferred_element_type=_F32)
        expand = (e_j == c * bpc + e_k).astype(_BF16)
        selx = jnp.dot(selb, expand, preferred_element_type=_F32)
        selx = jnp.concatenate([selx] * _NSA_G, axis=0)
        mask = (selx > 0.5) & ((c * chunk + k_i) <= tpos)
        s = jnp.where(mask, s, _NEG)
        m_new = jnp.maximum(m, jnp.max(s, -1, keepdims=True))
        p = jnp.where(mask, jnp.exp(s - m_new), 0.0)
        alpha = jnp.exp(m - m_new)
        l = alpha * l + jnp.sum(p, -1, keepdims=True)
        acc = alpha * acc + jnp.dot(p.astype(_BF16), vs_ref[0, 0, c], preferred_element_type=_F32)
        return m_new, l, acc

    init = (jnp.full((rows, 1), _NEG, _F32), jnp.zeros((rows, 1), _F32), jnp.zeros((rows, _NSA_DH), _F32))
    _, l_s, acc_s = lax.fori_loop(0, qb // bpc + 1, chunk_step, init)
    o_s = acc_s / jnp.maximum(l_s, 1.0)

    wlen = _WINDOW + 2 * _QB
    w0 = pl.multiple_of(jnp.clip(qb * _QB - (_WINDOW + _QB), 0, seq - wlen), _QB)
    s_w = lax.dot_general(q, kw_ref[0, 0, pl.ds(w0, wlen), :], nt, preferred_element_type=_F32)
    kpos = w0 + lax.broadcasted_iota(jnp.int32, (1, wlen), 1)
    p_w = _masked_softmax(s_w, (kpos <= tpos) & (kpos > tpos - _WINDOW))
    o_w = jnp.dot(p_w.astype(_BF16), vw_ref[0, 0, pl.ds(w0, wlen), :], preferred_element_type=_F32)

    g = g_ref[0, 0, 0]
    o_ref[0, 0, 0] = g[:, 0:1] * o_c + g[:, 1:2] * o_s + g[:, 2:3] * o_w


def _nsa_prompt(nq, ng, kcb, vcb, ks, vs, kw, vw):
    b, t = nq.shape[:2]
    nqb = t // _QB
    rows = _NSA_G * _QB
    chunk = 512
    q = (nq * (_NSA_DH ** -0.5)).astype(_BF16).reshape(b, nqb, _QB, _NSA_KVH, _NSA_G, _NSA_DH)
    q = q.transpose(0, 3, 1, 4, 2, 5).reshape(b, _NSA_KVH, nqb, rows, _NSA_DH)
    g = ng.reshape(b, nqb, _QB, _NSA_KVH, _NSA_G, 3).transpose(0, 3, 1, 4, 2, 5).reshape(b, _NSA_KVH, nqb, rows, 3)
    heads = lambda a: a.astype(_BF16).transpose(0, 2, 1, 3)
    kc, vc, kwh, vwh = heads(kcb), heads(vcb), heads(kw), heads(vw)
    ksc = heads(ks).reshape(b, _NSA_KVH, t // chunk, chunk, _NSA_DH)
    vsc = heads(vs).reshape(b, _NSA_KVH, t // chunk, chunk, _NSA_DH)
    n_cmp = kc.shape[2]
    per_q = lambda last: pl.BlockSpec((1, 1, 1, rows, last), lambda i, k, j: (i, k, j, 0, 0))
    per_bk = lambda a: pl.BlockSpec((1, 1) + a.shape[2:], lambda i, k, j: (i, k) + (0,) * (a.ndim - 2))
    o = pl.pallas_call(
        functools.partial(_nsa_prompt_body, seq=t),
        grid=(b, _NSA_KVH, nqb),
        in_specs=[per_q(_NSA_DH), per_q(3), per_bk(kc), per_bk(vc), per_bk(ksc), per_bk(vsc),
                  per_bk(kwh), per_bk(vwh)],
        out_specs=per_q(_NSA_DH),
        out_shape=jax.ShapeDtypeStruct((b, _NSA_KVH, nqb, rows, _NSA_DH), _F32),
        compiler_params=_cparams(("parallel", "parallel", "arbitrary")),
        name="nsa_prompt",
    )(q, g, kc, vc, ksc, vsc, kwh, vwh)
    o = o.reshape(b, _NSA_KVH, nqb, _NSA_G, _QB, _NSA_DH).transpose(0, 2, 4, 1, 3, 5)
    return o.reshape(b * t, _NSA_Q)


def _topk_rows(work, k):
    r = work.shape[0]
    iota = lax.broadcasted_iota(jnp.int32, work.shape, 0)
    vals = []
    for _ in range(k):
        mx = jnp.max(work, axis=0, keepdims=True)
        idx = jnp.min(jnp.where(work == mx, iota, r), axis=0, keepdims=True)
        work = jnp.where(iota == idx, -jnp.inf, work)
        vals.append(mx)
    return vals


def _peer_select_body(x_ref, sc_ref, sh_ref, wq_ref, sk_ref, xb_ref, st_ref, thr_ref, mz_ref, sv_scr):
    xb = (x_ref[...] * (1.0 + sc_ref[0]) + sh_ref[0]).astype(_BF16)
    xb_ref[...] = xb
    q = jnp.dot(xb, wq_ref[...], preferred_element_type=_F32).astype(_BF16)
    nt = (((1,), (1,)), ((), ()))
    k = _PEER_TOPK
    neg = jnp.full((1, q.shape[0]), -jnp.inf, _F32)
    for h in range(_PEER_H):
        for p in range(2):
            hp = 2 * h + p
            st = lax.dot_general(sk_ref[hp], q[:, hp * _PEER_HALF:(hp + 1) * _PEER_HALF], nt,
                                 preferred_element_type=_F32)
            st_ref[hp] = st
            for r, v in enumerate(_topk_rows(st, k)):
                sv_scr[p, r:r + 1, :] = v
        sv0, sv1 = sv_scr[0], sv_scr[1]
        row8 = lax.broadcasted_iota(jnp.int32, (8, 1), 0)
        row16 = lax.broadcasted_iota(jnp.int32, (k, 1), 0)
        groups = [sv0[0:1] + sv1, jnp.where(row16 == 0, neg, sv0 + sv1[0:1])]
        for a in range(1, 8):
            groups.append(jnp.where(row8 == 0, neg, sv0[a:a + 1] + sv1[0:8]))
        top = _topk_rows(jnp.concatenate(groups, axis=0), k)
        m = top[0]
        z = functools.reduce(lambda u, v: u + v, [jnp.exp(t - m) for t in top])
        thr_ref[h:h + 1, :] = top[k - 1]
        mz_ref[h:h + 1, :] = m + jnp.log(z)


def _peer_select(x1, sc3, sh3, mod_map, wq, subkeys, tn):
    n = x1.shape[0]
    r = sc3.shape[1]
    hp = 2 * _PEER_H
    wqb = wq.astype(_BF16)
    sk = subkeys.astype(_BF16).reshape(hp, _PEER_NK, _PEER_HALF)
    mod = pl.BlockSpec((1, r, _D), lambda i: (mod_map(i), 0, 0))
    return pl.pallas_call(
        _peer_select_body,
        grid=(n // tn,),
        in_specs=[pl.BlockSpec((tn, _D), lambda i: (i, 0)), mod, mod,
                  pl.BlockSpec(wqb.shape, lambda i: (0, 0)), pl.BlockSpec(sk.shape, lambda i: (0, 0, 0))],
        out_specs=[pl.BlockSpec((tn, _D), lambda i: (i, 0)),
                   pl.BlockSpec((hp, _PEER_NK, tn), lambda i: (0, 0, i)),
                   pl.BlockSpec((_PEER_H, tn), lambda i: (0, i)),
                   pl.BlockSpec((_PEER_H, tn), lambda i: (0, i))],
        out_shape=[jax.ShapeDtypeStruct((n, _D), _BF16),
                   jax.ShapeDtypeStruct((hp, _PEER_NK, n), _F32),
                   jax.ShapeDtypeStruct((_PEER_H, n), _F32),
                   jax.ShapeDtypeStruct((_PEER_H, n), _F32)],
        scratch_shapes=[pltpu.VMEM((2, _PEER_TOPK, tn), _F32)],
        compiler_params=_cparams(("parallel",)),
        name="peer_select",
    )(x1, sc3, sh3, wqb, sk)


def _peer_main_body(xb_ref, u_ref, v_ref, st_ref, thr_ref, mz_ref, x_ref, g2_ref, lng_ref, lnb_ref,
                    out_ref, acc_ref):
    e = pl.program_id(1)

    @pl.when(e == 0)
    def _():
        acc_ref[...] = jnp.zeros_like(acc_ref)

    nt = (((1,), (1,)), ((), ()))
    tn_dims = (((0,), (0,)), ((), ()))
    act = _gelu(lax.dot_general(u_ref[...], xb_ref[...], nt, preferred_element_type=_F32))
    slabs = u_ref.shape[0] // _PEER_NK
    gates = []
    for sl in range(slabs):
        i = e * slabs + sl
        gt = jnp.zeros((_PEER_NK, act.shape[1]), _F32)
        for h in range(_PEER_H):
            c = st_ref[2 * h, pl.ds(i, 1), :] + st_ref[2 * h + 1]
            gt = gt + jnp.where(c >= thr_ref[h:h + 1, :], jnp.exp(c - mz_ref[h:h + 1, :]), 0.0)
        gates.append(gt)
    wt = (act * jnp.concatenate(gates, axis=0)).astype(_BF16)
    acc_ref[...] += lax.dot_general(wt, v_ref[...], tn_dims, preferred_element_type=_F32)

    @pl.when(e == pl.num_programs(1) - 1)
    def _():
        r = _ALPHA * x_ref[...] + g2_ref[0] * acc_ref[...]
        out_ref[...] = _layer_norm(r, lng_ref[...], lnb_ref[...])


def _peer_main(xb, u_b, v_b, st, thr, mz, x1, g23, mod_map, ln_g, ln_b, tn, ec):
    n = x1.shape[0]
    r = g23.shape[1]
    ne = u_b.shape[0]
    hp = 2 * _PEER_H
    lng, lnb = ln_g.reshape(1, _D), ln_b.reshape(1, _D)
    return pl.pallas_call(
        _peer_main_body,
        grid=(n // tn, ne // ec),
        in_specs=[pl.BlockSpec((tn, _D), lambda i, e: (i, 0)),
                  pl.BlockSpec((ec, _D), lambda i, e: (e, 0)),
                  pl.BlockSpec((ec, _D), lambda i, e: (e, 0)),
                  pl.BlockSpec((hp, _PEER_NK, tn), lambda i, e: (0, 0, i)),
                  pl.BlockSpec((_PEER_H, tn), lambda i, e: (0, i)),
                  pl.BlockSpec((_PEER_H, tn), lambda i, e: (0, i)),
                  pl.BlockSpec((tn, _D), lambda i, e: (i, 0)),
                  pl.BlockSpec((1, r, _D), lambda i, e: (mod_map(i), 0, 0)),
                  pl.BlockSpec((1, _D), lambda i, e: (0, 0)),
                  pl.BlockSpec((1, _D), lambda i, e: (0, 0))],
        out_specs=pl.BlockSpec((tn, _D), lambda i, e: (i, 0)),
        out_shape=jax.ShapeDtypeStruct((n, _D), _F32),
        scratch_shapes=[pltpu.VMEM((tn, _D), _F32)],
        compiler_params=_cparams(("parallel", "arbitrary")),
        name="peer_main",
    )(xb, u_b, v_b, st, thr, mz, x1, g23, lng, lnb)


def _ret_log_decay():
    return jnp.log1p(-jnp.exp2(-5.0 - jnp.arange(_RET_H, dtype=_F32)))


def _retention_chunk(R, q, k, v, log_g):
    C = q.shape[2]
    n = jnp.arange(C, dtype=_F32)
    lg = log_g[:, None, None]
    diff = n[:, None] - n[None, :]
    decay = jnp.where(diff >= 0, jnp.exp(lg * jnp.maximum(diff, 0.0)), 0.0)
    inner = jnp.einsum('bhnm,bhme->bhne', jnp.einsum('bhnd,bhmd->bhnm', q, k) * decay, v)
    cross = jnp.einsum('bhnd,bhde->bhne', q, R) * jnp.exp(lg * (n[:, None] + 1.0))
    k_dec = k * jnp.exp(lg * (C - 1.0 - n[:, None]))
    R_new = R * jnp.exp(log_g * C)[:, None, None] + jnp.einsum('bhmd,bhme->bhde', k_dec, v)
    return R_new, inner + cross


def _retention_prompt(q, k, v):
    B, T, H, dk = q.shape
    C = min(_RET_C, T)
    nC = T // C
    log_g = _ret_log_decay()
    to_chunks = lambda a: a.reshape(B, nC, C, H, a.shape[-1]).transpose(1, 0, 3, 2, 4)
    step = lambda R, qkv: _retention_chunk(R, qkv[0], qkv[1], qkv[2], log_g)
    R0 = jnp.zeros((B, H, dk, v.shape[-1]), _F32)
    R, o = lax.scan(step, R0, (to_chunks(q), to_chunks(k), to_chunks(v)))
    return o.transpose(1, 0, 3, 2, 4).reshape(B, T, H, -1), R


def _compress_halves(rows, w1):
    B, L = rows.shape[:2]
    hv = rows.reshape(B, L // _CMP_STRIDE, _CMP_STRIDE, _NSA_KVH, _NSA_DH)
    w = w1.reshape(2, _CMP_STRIDE, _NSA_DH, _CMP_HID)
    return jnp.einsum('bnskd,psde->bnkpe', hv, w)


def _compress_blocks(z, pe, w1, w2):
    bias = pe.reshape(-1) @ w1
    hid = jax.nn.gelu(z[:, :-1, :, 0] + z[:, 1:, :, 1] + bias, approximate=False)
    return hid @ w2


def _nsa_attend_ref(q, qpos, gates, kc, vc, gather_sel, n_sel, kw, vw, kwpos):
    B, T = q.shape[:2]
    qf = q.astype(_F32) * _NSA_DH ** -0.5
    n_cmp = kc.shape[1]
    c_start = jnp.arange(n_cmp) * _CMP_STRIDE
    c_ok = (c_start + _CMP_BLOCK - 1)[None, :] <= qpos[:, None]
    p_c = _masked_softmax(jnp.einsum('btkgd,bnkd->bkgtn', qf, kc), c_ok)
    o_c = jnp.einsum('bkgtn,bnkd->btkgd', p_c, vc)
    j = jnp.arange(n_sel)
    overlap = ((c_start[:, None] < (j[None, :] + 1) * _SEL_BLOCK)
               & (c_start[:, None] + _CMP_BLOCK > j[None, :] * _SEL_BLOCK)).astype(_F32)
    imp = jnp.einsum('bkgtn,ns->bkts', p_c, overlap)
    cur = (qpos // _SEL_BLOCK)[:, None]
    forced = (j == 0) | (j == cur) | (j == cur - 1)
    score = jnp.where(j <= cur, jnp.where(forced, _FORCED, imp), -1.0)
    top_score, idx = lax.top_k(score, min(_SEL_TOPN, n_sel))
    ks, vs = gather_sel(idx)
    m = idx.shape[-1] * _SEL_BLOCK
    tok = idx[..., None] * _SEL_BLOCK + jnp.arange(_SEL_BLOCK)
    s_ok = (top_score >= 0)[..., None] & (tok <= qpos[:, None, None])
    ks = ks.reshape(B, _NSA_KVH, T, m, _NSA_DH)
    vs = vs.reshape(B, _NSA_KVH, T, m, _NSA_DH)
    p_s = _masked_softmax(jnp.einsum('btkgd,bktmd->bkgtm', qf, ks), s_ok.reshape(B, _NSA_KVH, 1, T, m))
    o_s = jnp.einsum('bkgtm,bktmd->btkgd', p_s, vs)
    w_ok = ((kwpos[None, :] <= qpos[:, None]) & (kwpos[None, :] > qpos[:, None] - _WINDOW)
            & (kwpos[None, :] >= 0))
    p_w = _masked_softmax(jnp.einsum('btkgd,bskd->bkgts', qf, kw), w_ok)
    o_w = jnp.einsum('bkgts,bskd->btkgd', p_w, vw)
    g = gates
    return g[..., 0:1] * o_c + g[..., 1:2] * o_s + g[..., 2:3] * o_w


def _split_y(y, b, t):
    sl = lambda c, w: y[:, c:c + w]
    kv = lambda c: sl(c, _NSA_KV).reshape(b, t, _NSA_KVH, _NSA_DH)
    return dict(
        rq=sl(_C_RQ, _RET_QK).reshape(b, t, _RET_H, _RET_DK), rk=sl(_C_RK, _RET_QK).reshape(b, t, _RET_H, _RET_DK),
        rv=sl(_C_RV, _RET_V).reshape(b, t, _RET_H, _RET_DV),
        nq=sl(_C_NQ, _NSA_Q).reshape(b, t, _NSA_H, _NSA_DH),
        kc=kv(_C_KC), vc=kv(_C_VC), ks=kv(_C_KS), vs=kv(_C_VS), kw=kv(_C_KW), vw=kv(_C_VW),
        ng=sl(_C_NG, 3 * _NSA_H).reshape(b, t, _NSA_KVH, _NSA_G, 3))


def _peer_and_ln(x1, sc3, sh3, g3, mod_map, wq, subkeys, u_b, v_b, ln_g, ln_b, tn):
    xb, st, thr, mz = _peer_select(x1, sc3, sh3, mod_map, wq, subkeys, tn)
    return _peer_main(xb, u_b, v_b, st, thr, mz, x1, g3, mod_map, ln_g, ln_b, tn, 512)


def kernel(x_prompt, x_sample, c_prompt, c_sample, state_ret, cache_cmp_k, cache_cmp_v, cache_sel_k, cache_sel_v,
           cache_win_k, cache_win_v, page_table, w_ada, b_ada, w_in, cmp_pe_k, cmp_w1_k, cmp_w2_k, cmp_pe_v,
           cmp_w1_v, cmp_w2_v, w_br_ret, w_br_nsa, w_out, ln1_g, ln1_b, peer_wq, peer_subkeys, peer_u, peer_v,
           ln2_g, ln2_b):
    l = 0
    B, T, _ = x_prompt.shape
    Bs, Ts, _ = x_sample.shape
    n_pages = page_table.shape[1]
    past = n_pages * _PAGE
    wcat = _build_wcat(w_in[l])
    u_b, v_b = peer_u[l].astype(_BF16), peer_v[l].astype(_BF16)

    tm = 512
    tpb = T // tm
    ada_p = _linear(c_prompt, w_ada[l], b_ada[l])
    sh1, sc1, g1, sh2, sc2, g2 = [a.reshape(B, 1, _D) for a in jnp.split(ada_p, 6, axis=-1)]
    xp2 = x_prompt.reshape(B * T, _D)
    tabs_p = _rope_tables(jnp.arange(T, dtype=jnp.int32))
    bmap = lambda i: i // tpb
    y = _inproj(xp2, sc1, sh1, bmap, wcat, tabs_p, lambda i: i % tpb, tm)
    s = _split_y(y, B, T)
    o_r, R_p = _retention_prompt(s['rq'], s['rk'], s['rv'])
    kcb = _compress_blocks(_compress_halves(s['kc'], cmp_w1_k[l]), cmp_pe_k[l], cmp_w1_k[l], cmp_w2_k[l])
    vcb = _compress_blocks(_compress_halves(s['vc'], cmp_w1_v[l]), cmp_pe_v[l], cmp_w1_v[l], cmp_w2_v[l])
    padc = ((0, 0), (0, 1), (0, 0), (0, 0))
    o_n = _nsa_prompt(s['nq'], s['ng'], jnp.pad(kcb, padc), jnp.pad(vcb, padc), s['ks'], s['vs'], s['kw'], s['vw'])
    tm2 = 256
    tpb2 = T // tm2
    bmap2 = lambda i: i // tpb2
    x1 = _merge(o_r.reshape(B * T, _RET_V), y, o_n, xp2, g1, bmap2, w_br_ret[l], w_br_nsa[l], w_out[l],
                ln1_g[l], ln1_b[l], tm2)
    xp_out = _peer_and_ln(x1, sc2, sh2, g2, bmap2, peer_wq[l], peer_subkeys[l], u_b, v_b, ln2_g[l], ln2_b[l], tm2)
    wb = min(_WINDOW, T)
    prompt_state = (R_p, s['kc'], s['vc'], s['ks'], s['vs'], s['kw'][:, T - wb:], s['vw'][:, T - wb:])

    ns = Bs * Ts
    tms = min(512, ns)
    ada_s = _linear(c_sample, w_ada[l], b_ada[l])
    rowmod = lambda a: jnp.broadcast_to(a[:, None, :], (Bs, Ts, _D)).reshape(ns // tms, tms, _D)
    sh1, sc1, g1, sh2, sc2, g2 = [rowmod(a) for a in jnp.split(ada_s, 6, axis=-1)]
    xs2 = x_sample.reshape(ns, _D)
    pos = past + jnp.arange(Ts, dtype=jnp.int32)
    tabs_s = _rope_tables(jnp.tile(pos, tms // Ts))
    ident = lambda i: i
    ys = _inproj(xs2, sc1, sh1, ident, wcat, tabs_s, lambda i: 0, tms)
    s = _split_y(ys, Bs, Ts)
    r_new, o_r = _retention_chunk(state_ret[l], jnp.swapaxes(s['rq'], 1, 2), jnp.swapaxes(s['rk'], 1, 2),
                                  jnp.swapaxes(s['rv'], 1, 2), _ret_log_decay())
    o_r = jnp.swapaxes(o_r, 1, 2)
    n_new16 = (Ts // _CMP_STRIDE) * _CMP_STRIDE

    def summaries(pool, new, pe, w1, w2):
        rows = pool[page_table].reshape(Bs, past, _NSA_KVH, _NSA_DH)
        z = jnp.concatenate([_compress_halves(rows, w1), _compress_halves(new[:, :n_new16], w1)], axis=1)
        return _compress_blocks(z, pe, w1, w2)

    kcb = summaries(cache_cmp_k[l], s['kc'], cmp_pe_k[l], cmp_w1_k[l], cmp_w2_k[l])
    vcb = summaries(cache_cmp_v[l], s['vc'], cmp_pe_v[l], cmp_w1_v[l], cmp_w2_v[l])
    n_sel = -(-(past + Ts) // _SEL_BLOCK)
    past_blocks = past // _SEL_BLOCK
    per_page = _PAGE // _SEL_BLOCK
    n_new_blocks = -(-Ts // _SEL_BLOCK)
    bi = jnp.arange(Bs)[:, None, None, None]
    hi = jnp.arange(_NSA_KVH)[None, :, None, None]

    def new_blocks(a):
        a = jnp.pad(a, ((0, 0), (0, n_new_blocks * _SEL_BLOCK - Ts), (0, 0), (0, 0)))
        return a.reshape(Bs, n_new_blocks, _SEL_BLOCK, _NSA_KVH, _NSA_DH)

    sel_src = ((cache_sel_k[l].reshape(-1, _SEL_BLOCK, _NSA_KVH, _NSA_DH), new_blocks(s['ks'])),
               (cache_sel_v[l].reshape(-1, _SEL_BLOCK, _NSA_KVH, _NSA_DH), new_blocks(s['vs'])))

    def gather_sel(idx):
        page = jnp.minimum(idx // per_page, n_pages - 1)
        phys = page_table[bi, page] * per_page + idx % per_page
        jn = jnp.clip(idx - past_blocks, 0, n_new_blocks - 1)
        in_past = (idx < past_blocks)[..., None, None]
        return tuple(jnp.where(in_past, pool[phys, :, hi, :], new[bi, jn, :, hi, :]) for pool, new in sel_src)

    kwin = jnp.concatenate([cache_win_k[l], s['kw']], axis=1)
    vwin = jnp.concatenate([cache_win_v[l], s['vw']], axis=1)
    wbs = cache_win_k.shape[2]
    kwpos = past - wbs + jnp.arange(wbs + Ts, dtype=jnp.int32)
    nq5 = s['nq'].reshape(Bs, Ts, _NSA_KVH, _NSA_G, _NSA_DH)

    def one_query(args):
        qt, gt, pt = args
        return _nsa_attend_ref(qt[:, None], pt[None], gt[:, None], kcb, vcb, gather_sel, n_sel, kwin, vwin, kwpos)[:, 0]

    o_n = jnp.swapaxes(lax.map(one_query, (jnp.swapaxes(nq5, 0, 1), jnp.swapaxes(s['ng'], 0, 1), pos)), 0, 1)
    tms2 = min(256, ns)
    rowmod2 = lambda a: a.reshape(ns // tms2, tms2, _D)
    x1s = _merge(o_r.reshape(ns, _RET_V), ys, o_n.reshape(ns, _NSA_Q), xs2, rowmod2(g1), ident, w_br_ret[l],
                 w_br_nsa[l], w_out[l], ln1_g[l], ln1_b[l], tms2)
    xs_out = _peer_and_ln(x1s, rowmod2(sc2), rowmod2(sh2), rowmod2(g2), ident, peer_wq[l], peer_subkeys[l],
                          u_b, v_b, ln2_g[l], ln2_b[l], tms2)
    sample_state = (r_new, s['kc'], s['vc'], s['ks'], s['vs'], kwin[:, Ts:], vwin[:, Ts:])

    outs = [xp_out.reshape(B, T, _D), xs_out.reshape(Bs, Ts, _D)]
    outs += [a[None] for a in prompt_state]
    outs += [a[None] for a in sample_state]
    return tuple(outs)
```

```python
import functools
import math

import jax
import jax.numpy as jnp
import numpy as np
from jax import lax
from jax.experimental import pallas as pl
from jax.experimental.pallas import tpu as pltpu

_F32 = jnp.float32
_BF16 = jnp.bfloat16

_D = 1024
_RET_H, _RET_DK, _RET_DV, _RET_C = 4, 128, 256, 128
_NSA_H, _NSA_KVH, _NSA_G, _NSA_DH = 8, 2, 4, 64
_CMP_BLOCK, _CMP_STRIDE, _CMP_HID = 32, 16, 128
_SEL_BLOCK, _SEL_TOPN, _WINDOW, _QB = 64, 16, 512, 64
_FORCED = 1.0e4
_PEER_H, _PEER_NK, _PEER_HALF, _PEER_TOPK = 8, 128, 128, 16
_PAGE = 128
_ALPHA = 2.0 ** 0.25
_LN_EPS = 1e-5
_NEG = -1e30
_ROPE_THETA = 10000.0
_RET_QK = _RET_H * _RET_DK
_RET_V = _RET_H * _RET_DV
_NSA_Q = _NSA_H * _NSA_DH
_NSA_KV = _NSA_KVH * _NSA_DH
_IN_SECTIONS = (_RET_QK, _RET_QK, _RET_V, _RET_V, _NSA_Q, 6 * _NSA_KV, 3 * _NSA_H, 2 * _D)
_IN_SPLITS = tuple(int(s) for s in np.cumsum(_IN_SECTIONS)[:-1])
_RET_LOG_G = tuple(math.log1p(-2.0 ** (-5.0 - h)) for h in range(_RET_H))

_VMEM_LIMIT = 48 * 1024 * 1024

_TN = 512
_C_RQ, _C_RK, _C_RV, _C_RG, _C_MG, _C_NQ = 0, 512, 1024, 2048, 3072, 5120
_C_KC, _C_KS, _C_KW = 5632, 5760, 5888
_C_VC, _C_VS, _C_VW, _C_NG = 6144, 6272, 6400, 6528
_IN_COLS = 13 * _TN

_SEL_CHUNK = 512
_NSA_ROWS = _NSA_KVH * _NSA_G * _QB

_NT = (((1,), (1,)), ((), ()))
_TN_DIMS = (((0,), (0,)), ((), ()))


def _cparams(sem):
    return pltpu.CompilerParams(dimension_semantics=sem, vmem_limit_bytes=_VMEM_LIMIT)


def _gelu(x):
    return 0.5 * x * (1.0 + lax.erf(x * (1.0 / math.sqrt(2.0))))


def _layer_norm(r, g, b):
    mu = jnp.mean(r, -1, keepdims=True)
    d = r - mu
    var = jnp.mean(d * d, -1, keepdims=True)
    return d * lax.rsqrt(var + _LN_EPS) * g + b


def _masked_softmax(s, mask):
    s = jnp.where(mask, s, _NEG)
    p = jnp.where(mask, jnp.exp(s - jnp.max(s, -1, keepdims=True)), 0.0)
    return p / jnp.maximum(jnp.sum(p, -1, keepdims=True), 1.0)


def _full_spec(a, grid_rank):
    zeros = (0,) * a.ndim
    return pl.BlockSpec(a.shape, {1: lambda i: zeros, 2: lambda i, j: zeros}[grid_rank])


def _linear_body(a_ref, w_ref, b_ref, o_ref):
    o_ref[...] = jnp.dot(a_ref[...].astype(_BF16), w_ref[...], preferred_element_type=_F32) + b_ref[...]


def _linear(a, w, b, tn=512):
    m, k = a.shape
    n = w.shape[1]
    return pl.pallas_call(
        _linear_body,
        grid=(n // tn,),
        in_specs=[pl.BlockSpec((m, k), lambda j: (0, 0)),
                  pl.BlockSpec((k, tn), lambda j: (0, j)),
                  pl.BlockSpec((1, tn), lambda j: (0, j))],
        out_specs=pl.BlockSpec((m, tn), lambda j: (0, j)),
        out_shape=jax.ShapeDtypeStruct((m, n), _F32),
        compiler_params=_cparams(("arbitrary",)),
        name="ada_linear",
    )(a, w.astype(_BF16), b.reshape(1, n))


def _build_wcat(w_in):
    rq, rk, rv, rg, nq, nkv, ng, mg = jnp.split(w_in, _IN_SPLITS, axis=1)
    nkv6 = nkv.reshape(_D, 6, _NSA_KV)
    kc, vc, ks, vs, kw, vw = [nkv6[:, i] for i in range(6)]
    z = jnp.zeros((_D, _NSA_KV), w_in.dtype)
    ngp = jnp.pad(ng, ((0, 0), (0, _NSA_KV - ng.shape[1])))
    wcat = jnp.concatenate([rq, rk, rv, rg, mg, nq, kc, ks, kw, z, vc, vs, vw, ngp], axis=1)
    return wcat.astype(_BF16)


def _rope_tables(pos):
    out = []
    for half in (_RET_DK // 2, _NSA_DH // 2):
        inv = _ROPE_THETA ** (-jnp.arange(half, dtype=_F32) / half)
        ang = pos.astype(_F32)[:, None] * inv[None, :]
        cos, sin = jnp.cos(ang), jnp.sin(ang)
        reps = _TN // (2 * half)
        out.append(jnp.tile(jnp.concatenate([cos, cos], -1), (1, reps)))
        out.append(jnp.tile(jnp.concatenate([-sin, sin], -1), (1, reps)))
    return out


def _inproj_body(x_ref, sc_ref, sh_ref, w_ref, c128_ref, s128_ref, c64_ref, s64_ref, y_ref, a_scr):
    j = pl.program_id(1)

    @pl.when(j == 0)
    def _():
        a_scr[...] = (x_ref[...] * (1.0 + sc_ref[0]) + sh_ref[0]).astype(_BF16)

    y = jnp.dot(a_scr[...], w_ref[...], preferred_element_type=_F32)
    width = y.shape[1]
    lane = lax.broadcasted_iota(jnp.int32, y.shape, 1)

    def rope(half, c_ref, s_ref):
        first = (lane % (2 * half)) < half
        partner = jnp.where(first, pltpu.roll(y, width - half, 1), pltpu.roll(y, half, 1))
        return y * c_ref[...] + partner * s_ref[...]

    @pl.when(j == _C_RQ // _TN)
    def _():
        y_ref[...] = rope(_RET_DK // 2, c128_ref, s128_ref)

    @pl.when(j == _C_RK // _TN)
    def _():
        y_ref[...] = rope(_RET_DK // 2, c128_ref, s128_ref) * (_RET_DK ** -0.5)

    @pl.when((j >= _C_RV // _TN) & (j < _C_RG // _TN))
    def _():
        y_ref[...] = y

    @pl.when((j >= _C_RG // _TN) & (j < _C_MG // _TN))
    def _():
        y_ref[...] = y * jax.nn.sigmoid(y)

    @pl.when((j >= _C_MG // _TN) & (j < _C_NQ // _TN))
    def _():
        y_ref[...] = jax.nn.sigmoid(y)

    @pl.when((j >= _C_NQ // _TN) & (j < _C_VC // _TN))
    def _():
        y_ref[...] = rope(_NSA_DH // 2, c64_ref, s64_ref)

    @pl.when(j == _C_VC // _TN)
    def _():
        y_ref[...] = jnp.where(lane < _C_NG - _C_VC, y, jax.nn.sigmoid(y))


def _inproj(x2, sc3, sh3, mod_map, wcat, tabs, tab_map, tm):
    n = x2.shape[0]
    r = sc3.shape[1]
    tab_spec = pl.BlockSpec((tm, _TN), lambda i, j: (tab_map(i), 0))
    return pl.pallas_call(
        _inproj_body,
        grid=(n // tm, _IN_COLS // _TN),
        in_specs=[pl.BlockSpec((tm, _D), lambda i, j: (i, 0)),
                  pl.BlockSpec((1, r, _D), lambda i, j: (mod_map(i), 0, 0)),
                  pl.BlockSpec((1, r, _D), lambda i, j: (mod_map(i), 0, 0)),
                  pl.BlockSpec((_D, _TN), lambda i, j: (0, j)),
                  tab_spec, tab_spec, tab_spec, tab_spec],
        out_specs=pl.BlockSpec((tm, _TN), lambda i, j: (i, j)),
        out_shape=jax.ShapeDtypeStruct((n, _IN_COLS), _F32),
        scratch_shapes=[pltpu.VMEM((tm, _D), _BF16)],
        compiler_params=_cparams(("parallel", "arbitrary")),
        name="in_proj",
    )(x2, sc3, sh3, wcat, *tabs)


def _ret_consts(c):
    lg = jnp.asarray(_RET_LOG_G, _F32)[:, None, None]
    n = jnp.arange(c, dtype=_F32)
    diff = n[:, None] - n[None, :]
    decay = jnp.where(diff >= 0, jnp.exp(lg * jnp.maximum(diff, 0.0)), 0.0)
    cs = jnp.broadcast_to(jnp.exp(lg * (n[:, None] + 1.0)), (_RET_H, c, _RET_DV))
    kd = jnp.broadcast_to(jnp.exp(lg * (c - 1.0 - n[:, None])), (_RET_H, c, _RET_DK))
    return decay, cs, kd


def _ret_step(q_ref, k_ref, v_ref, dec_ref, cs_ref, kd_ref, o_ref, state, chunk):
    new = []
    for h in range(_RET_H):
        q = q_ref[:, h * _RET_DK:(h + 1) * _RET_DK].astype(_BF16)
        k = k_ref[:, h * _RET_DK:(h + 1) * _RET_DK]
        v = v_ref[:, h * _RET_DV:(h + 1) * _RET_DV].astype(_BF16)
        r = state(h)
        s = lax.dot_general(q, k.astype(_BF16), _NT, preferred_element_type=_F32) * dec_ref[h]
        inner = jnp.dot(s.astype(_BF16), v, preferred_element_type=_F32)
        cross = jnp.dot(q, r.astype(_BF16), preferred_element_type=_F32) * cs_ref[h]
        o_ref[:, h * _RET_DV:(h + 1) * _RET_DV] = inner + cross
        k_dec = (k * kd_ref[h]).astype(_BF16)
        new.append(r * math.exp(_RET_LOG_G[h] * chunk)
                   + lax.dot_general(k_dec, v, _TN_DIMS, preferred_element_type=_F32))
    return new


def _ret_prompt_body(q_ref, k_ref, v_ref, dec_ref, cs_ref, kd_ref, o_ref, r_ref):
    @pl.when(pl.program_id(1) == 0)
    def _():
        r_ref[...] = jnp.zeros_like(r_ref)

    new = _ret_step(q_ref, k_ref, v_ref, dec_ref, cs_ref, kd_ref, o_ref, lambda h: r_ref[0, h], _RET_C)
    for h in range(_RET_H):
        r_ref[0, h] = new[h]


def _retention_prompt(y, b, t):
    c = _RET_C
    nc = t // c
    consts = _ret_consts(c)
    return pl.pallas_call(
        _ret_prompt_body,
        grid=(b, nc),
        in_specs=[pl.BlockSpec((c, _RET_QK), lambda i, j: (i * nc + j, _C_RQ // _RET_QK)),
                  pl.BlockSpec((c, _RET_QK), lambda i, j: (i * nc + j, _C_RK // _RET_QK)),
                  pl.BlockSpec((c, _RET_V), lambda i, j: (i * nc + j, _C_RV // _RET_V))]
        + [_full_spec(a, 2) for a in consts],
        out_specs=[pl.BlockSpec((c, _RET_V), lambda i, j: (i * nc + j, 0)),
                   pl.BlockSpec((1, _RET_H, _RET_DK, _RET_DV), lambda i, j: (i, 0, 0, 0))],
        out_shape=[jax.ShapeDtypeStruct((b * t, _RET_V), _F32),
                   jax.ShapeDtypeStruct((b, _RET_H, _RET_DK, _RET_DV), _F32)],
        compiler_params=_cparams(("parallel", "arbitrary")),
        name="retention_prompt",
    )(y, y, y, *consts)


def _ret_sample_body(q_ref, k_ref, v_ref, st_ref, dec_ref, cs_ref, kd_ref, o_ref, r_ref, *, chunk):
    new = _ret_step(q_ref, k_ref, v_ref, dec_ref, cs_ref, kd_ref, o_ref, lambda h: st_ref[0, h], chunk)
    for h in range(_RET_H):
        r_ref[0, h] = new[h]


def _retention_sample(y, state, b, t):
    consts = _ret_consts(t)
    st_spec = pl.BlockSpec((1, _RET_H, _RET_DK, _RET_DV), lambda i: (i, 0, 0, 0))
    return pl.pallas_call(
        functools.partial(_ret_sample_body, chunk=t),
        grid=(b,),
        in_specs=[pl.BlockSpec((t, _RET_QK), lambda i: (i, _C_RQ // _RET_QK)),
                  pl.BlockSpec((t, _RET_QK), lambda i: (i, _C_RK // _RET_QK)),
                  pl.BlockSpec((t, _RET_V), lambda i: (i, _C_RV // _RET_V)),
                  st_spec] + [_full_spec(a, 1) for a in consts],
        out_specs=[pl.BlockSpec((t, _RET_V), lambda i: (i, 0)), st_spec],
        out_shape=[jax.ShapeDtypeStruct((b * t, _RET_V), _F32),
                   jax.ShapeDtypeStruct(state.shape, _F32)],
        compiler_params=_cparams(("parallel",)),
        name="retention_sample",
    )(y, y, y, state, *consts)


def _compress_weights(pe, w1, w2):
    w = w1.reshape(2, _CMP_STRIDE, _NSA_DH, _CMP_HID)
    eye = jnp.eye(_NSA_KVH, dtype=w1.dtype)
    wf = jnp.einsum('psde,kl->skdlpe', w, eye).reshape(_CMP_STRIDE * _NSA_KV, _NSA_KVH * 2 * _CMP_HID)
    pe8 = jnp.pad(pe.reshape(1, -1), ((0, 7), (0, 0)))
    return wf.astype(_BF16), pe8.astype(_BF16), w1.astype(_BF16), w2.astype(_BF16)


def _compress_compute(rows, wf_ref, pe_ref, w1_ref, w2_ref, o_ref, z_scr):
    ns = rows.shape[0]
    z = jnp.dot(rows.astype(_BF16), wf_ref[...], preferred_element_type=_F32)
    z_scr[pl.ds(0, ns), :] = z
    z_scr[pl.ds(ns, 8), :] = jnp.zeros((8, z.shape[1]), _F32)
    bias = jnp.dot(pe_ref[...], w1_ref[...], preferred_element_type=_F32)[0:1]
    for k in range(_NSA_KVH):
        c0 = k * 2 * _CMP_HID
        z_next = z_scr[pl.ds(1, ns), c0 + _CMP_HID:c0 + 2 * _CMP_HID]
        hid = _gelu(z[:, c0:c0 + _CMP_HID] + z_next + bias)
        o_ref[:, k * _NSA_DH:(k + 1) * _NSA_DH] = jnp.dot(hid.astype(_BF16), w2_ref[...],
                                                         preferred_element_type=_F32)


def _compress_body(rows_ref, wf_ref, pe_ref, w1_ref, w2_ref, o_ref, z_scr):
    _compress_compute(rows_ref[...], wf_ref, pe_ref, w1_ref, w2_ref, o_ref, z_scr)


def _compress_prompt(rows2, b, weights):
    ns = rows2.shape[0] // b
    return pl.pallas_call(
        _compress_body,
        grid=(b,),
        in_specs=[pl.BlockSpec((ns, rows2.shape[1]), lambda i: (i, 0))] + [_full_spec(a, 1) for a in weights],
        out_specs=pl.BlockSpec((ns, _NSA_KV), lambda i: (i, 0)),
        out_shape=jax.ShapeDtypeStruct((b * ns, _NSA_KV), _F32),
        scratch_shapes=[pltpu.VMEM((ns + 8, _NSA_KVH * 2 * _CMP_HID), _F32)],
        compiler_params=_cparams(("parallel",)),
        name="compress_prompt",
    )(rows2, *weights)


def _page_copies(pt_ref, b, pool_ref, rows_per_page, dst_ref, sem):
    n_pages = pt_ref.shape[1]
    return [pltpu.make_async_copy(pool_ref.at[pl.ds(pt_ref[b, p] * rows_per_page, rows_per_page)],
                                  dst_ref.at[pl.ds(p * rows_per_page, rows_per_page)], sem)
            for p in range(n_pages)]


def _compress_paged_body(pt_ref, pool_ref, wf_ref, pe_ref, w1_ref, w2_ref, o_ref, rows_scr, z_scr, sem):
    copies = _page_copies(pt_ref, pl.program_id(0), pool_ref, _PAGE // _CMP_STRIDE, rows_scr, sem)
    for cp in copies:
        cp.start()
    for cp in copies:
        cp.wait()
    _compress_compute(rows_scr[...], wf_ref, pe_ref, w1_ref, w2_ref, o_ref, z_scr)


def _compress_paged(pool2, page_table, weights):
    b, n_pages = page_table.shape
    ns = n_pages * (_PAGE // _CMP_STRIDE)
    grid_spec = pltpu.PrefetchScalarGridSpec(
        num_scalar_prefetch=1,
        grid=(b,),
        in_specs=[pl.BlockSpec(memory_space=pl.ANY)]
        + [pl.BlockSpec(a.shape, lambda i, pt, nd=a.ndim: (0,) * nd) for a in weights],
        out_specs=pl.BlockSpec((ns, _NSA_KV), lambda i, pt: (i, 0)),
        scratch_shapes=[pltpu.VMEM((ns, pool2.shape[1]), _F32),
                        pltpu.VMEM((ns + 8, _NSA_KVH * 2 * _CMP_HID), _F32),
                        pltpu.SemaphoreType.DMA],
    )
    return pl.pallas_call(
        _compress_paged_body,
        grid_spec=grid_spec,
        out_shape=jax.ShapeDtypeStruct((b * ns, _NSA_KV), _F32),
        compiler_params=_cparams(("arbitrary",)),
        name="compress_paged",
    )(page_table, pool2, *weights)


def _overlap_matrix(n_cmp, n_sel_pad):
    nn = lax.broadcasted_iota(jnp.int32, (n_cmp, n_sel_pad), 0)
    jj = lax.broadcasted_iota(jnp.int32, (n_cmp, n_sel_pad), 1)
    per = _SEL_BLOCK // _CMP_STRIDE
    return ((nn < (jj + 1) * per) & (nn * _CMP_STRIDE + _CMP_BLOCK > jj * _SEL_BLOCK)).astype(_BF16)


def _importance(pcs, ov):
    hi = pcs.astype(_BF16)
    lo = (pcs - hi.astype(_F32)).astype(_BF16)
    return jnp.dot(hi, ov, preferred_element_type=_F32) + jnp.dot(lo, ov, preferred_element_type=_F32)


def _topk_mask_rows(work, k):
    r = work.shape[0]
    iota = lax.broadcasted_iota(jnp.int32, work.shape, 0)
    sel = jnp.zeros(work.shape, _F32)
    for _ in range(k):
        mx = jnp.max(work, axis=0, keepdims=True)
        idx = jnp.min(jnp.where(work == mx, iota, r), axis=0, keepdims=True)
        pick = iota == idx
        sel = jnp.where(pick, 1.0, sel)
        work = jnp.where(pick, -jnp.inf, work)
    return sel


def _nsa_prompt_body(q_ref, g_ref, kc_ref, vc_ref, ks_ref, vs_ref, kw_ref, vw_ref, o_ref, *, seq):
    qb = pl.program_id(1)
    rows = _NSA_ROWS
    n_sel_pad = ks_ref.shape[3] - _NSA_KV
    q = q_ref[0, 0]
    row = lax.broadcasted_iota(jnp.int32, (rows, 1), 0)
    tpos = qb * _QB + row % _QB

    n_cmp = kc_ref.shape[1]
    s_c = lax.dot_general(q, kc_ref[0], _NT, preferred_element_type=_F32)
    n_i = lax.broadcasted_iota(jnp.int32, (1, n_cmp), 1)
    p_c = _masked_softmax(s_c, (n_i * _CMP_STRIDE + _CMP_BLOCK - 1) <= tpos)
    o_c = jnp.dot(p_c.astype(_BF16), vc_ref[0], preferred_element_type=_F32)

    gq = _NSA_G * _QB
    pcs = jnp.concatenate(
        [functools.reduce(lambda u, v: u + v, [p_c[k * gq + g * _QB:k * gq + (g + 1) * _QB] for g in range(_NSA_G)])
         for k in range(_NSA_KVH)], axis=0)
    imp = _importance(pcs, _overlap_matrix(n_cmp, n_sel_pad))
    j_i = lax.broadcasted_iota(jnp.int32, imp.shape, 1)
    forced = (j_i == 0) | (j_i == qb) | (j_i == qb - 1)
    score = jnp.where(j_i <= qb, jnp.where(forced, _FORCED, imp), -1.0)
    score_t = score.T
    sel = (_topk_mask_rows(score_t, _SEL_TOPN) * (score_t >= 0.0).astype(_F32)).T
    bias = jnp.where(sel > 0.5, 0.0, _NEG).astype(_BF16)
    bias = jnp.concatenate([bias[k * _QB:(k + 1) * _QB] for k in range(_NSA_KVH) for _ in range(_NSA_G)], axis=0)

    q_sel = jnp.concatenate([q, bias], axis=1)
    k_i = lax.broadcasted_iota(jnp.int32, (1, _SEL_CHUNK), 1)

    def chunk_step(c, carry, causal):
        m, l, acc = carry
        s = lax.dot_general(q_sel, ks_ref[0, c], _NT, preferred_element_type=_F32)
        if causal:
            s = jnp.where((c * _SEL_CHUNK + k_i) <= tpos, s, _NEG)
        m_new = jnp.maximum(m, jnp.max(s, -1, keepdims=True))
        p = jnp.exp(s - m_new)
        alpha = jnp.exp(m - m_new)
        l = alpha * l + jnp.sum(p, -1, keepdims=True)
        acc = alpha * acc + jnp.dot(p.astype(_BF16), vs_ref[0, c], preferred_element_type=_F32)
        return m_new, l, acc

    init = (jnp.full((rows, 1), _NEG, _F32), jnp.zeros((rows, 1), _F32), jnp.zeros((rows, _NSA_KV), _F32))
    last = qb // (_SEL_CHUNK // _SEL_BLOCK)
    carry = lax.fori_loop(0, last, lambda c, cr: chunk_step(c, cr, False), init)
    _, l_s, acc_s = chunk_step(last, carry, True)
    o_s = acc_s / jnp.maximum(l_s, 1.0)

    wlen = _WINDOW + 2 * _QB
    w0 = pl.multiple_of(jnp.clip(qb * _QB - (_WINDOW + _QB), 0, seq - wlen), _QB)
    s_w = lax.dot_general(q, kw_ref[0, pl.ds(w0, wlen), :], _NT, preferred_element_type=_F32)
    kpos = w0 + lax.broadcasted_iota(jnp.int32, (1, wlen), 1)
    p_w = _masked_softmax(s_w, (kpos <= tpos) & (kpos > tpos - _WINDOW))
    o_w = jnp.dot(p_w.astype(_BF16), vw_ref[0, pl.ds(w0, wlen), :], preferred_element_type=_F32)

    g = g_ref[0, 0]
    o_ref[0, 0] = g[:, 0:1] * o_c + g[:, 1:2] * o_s + g[:, 2:3] * o_w


def _block_diag_queries(nq):
    *lead, t, _, _ = nq.shape
    q = (nq * (_NSA_DH ** -0.5)).astype(_BF16).reshape(*lead, t, _NSA_KVH, _NSA_G, _NSA_DH)
    q = jnp.moveaxis(q, -4, -2)
    eye = jnp.eye(_NSA_KVH, dtype=_BF16)
    q = q[..., None, :] * eye[:, None, None, :, None]
    return q.reshape(*lead, _NSA_KVH * _NSA_G * t, _NSA_KV)


def _gate_rows(ng):
    *lead, t, _, _, _ = ng.shape
    return jnp.moveaxis(ng, -4, -2).reshape(*lead, _NSA_KVH * _NSA_G * t, 3)


def _own_head_rows(o, t):
    lead = o.shape[:-2]
    o = o.reshape(*lead, _NSA_KVH, _NSA_G, t, _NSA_KVH, _NSA_DH)
    o = jnp.stack([o[..., k, :, :, k, :] for k in range(_NSA_KVH)], axis=-4)
    return jnp.moveaxis(o, -2, -4).reshape(*lead, t, _NSA_Q)


def _nsa_prompt(y, kcb, vcb, b, t):
    nqb = t // _QB
    n_sel_pad = 128
    assert nqb <= n_sel_pad and t % _SEL_CHUNK == 0
    col = lambda c, w: y[:, c:c + w]
    q = _block_diag_queries(col(_C_NQ, _NSA_Q).reshape(b, nqb, _QB, _NSA_H, _NSA_DH))
    g = _gate_rows(col(_C_NG, 3 * _NSA_H).reshape(b, nqb, _QB, _NSA_KVH, _NSA_G, 3))
    seq_b = lambda c: col(c, _NSA_KV).astype(_BF16).reshape(b, t, _NSA_KV)
    chunked = lambda a: a.reshape(b, t // _SEL_CHUNK, _SEL_CHUNK, a.shape[-1])
    block_id = (jnp.arange(t)[:, None] // _SEL_BLOCK == jnp.arange(n_sel_pad)[None, :]).astype(_BF16)
    ks = chunked(jnp.concatenate([seq_b(_C_KS), jnp.broadcast_to(block_id, (b, t, n_sel_pad))], axis=-1))
    vs, kw, vw = chunked(seq_b(_C_VS)), seq_b(_C_KW), seq_b(_C_VW)
    kc, vc = kcb.astype(_BF16), vcb.astype(_BF16)
    per_q = lambda last: pl.BlockSpec((1, 1, _NSA_ROWS, last), lambda i, j: (i, j, 0, 0))
    per_b = lambda a: pl.BlockSpec((1,) + a.shape[1:], lambda i, j: (i,) + (0,) * (a.ndim - 1))
    o = pl.pallas_call(
        functools.partial(_nsa_prompt_body, seq=t),
        grid=(b, nqb),
        in_specs=[per_q(_NSA_KV), per_q(3), per_b(kc), per_b(vc), per_b(ks), per_b(vs), per_b(kw), per_b(vw)],
        out_specs=per_q(_NSA_KV),
        out_shape=jax.ShapeDtypeStruct((b, nqb, _NSA_ROWS, _NSA_KV), _F32),
        compiler_params=_cparams(("parallel", "arbitrary")),
        name="nsa_prompt",
    )(q, g, kc, vc, ks, vs, kw, vw)
    return _own_head_rows(o, _QB).reshape(b * t, _NSA_Q)


def _nsa_sample_body(pt_ref, q_ref, g_ref, kc_ref, vc_ref, ksn_ref, vsn_ref, kwc_ref, vwc_ref, kwn_ref, vwn_ref,
                     e_ref, poolk_ref, poolv_ref, o_ref, ksel_scr, vsel_scr, sem, *, past, ts):
    b = pl.program_id(0)
    copies = (_page_copies(pt_ref, b, poolk_ref, _PAGE, ksel_scr, sem.at[0])
              + _page_copies(pt_ref, b, poolv_ref, _PAGE, vsel_scr, sem.at[1]))
    for cp in copies:
        cp.start()

    rows = _NSA_KVH * _NSA_G * ts
    q = q_ref[0]
    row = lax.broadcasted_iota(jnp.int32, (rows, 1), 0)
    t_new = row % ts
    tpos = past + t_new
    new_ok = lax.broadcasted_iota(jnp.int32, (1, ts), 1) <= t_new

    n_cmp = kc_ref.shape[1]
    s_c = lax.dot_general(q, kc_ref[0], _NT, preferred_element_type=_F32)
    n_i = lax.broadcasted_iota(jnp.int32, (1, n_cmp), 1)
    p_c = _masked_softmax(s_c, (n_i * _CMP_STRIDE + _CMP_BLOCK - 1) <= tpos)
    o_c = jnp.dot(p_c.astype(_BF16), vc_ref[0], preferred_element_type=_F32)

    n_past = e_ref.shape[0]
    gt = _NSA_G * ts
    pcs = jnp.concatenate(
        [functools.reduce(lambda u, v: u + v, [p_c[k * gt + g * ts:k * gt + (g + 1) * ts] for g in range(_NSA_G)])
         for k in range(_NSA_KVH)], axis=0)
    imp = _importance(pcs, _overlap_matrix(n_cmp, n_past))
    j_i = lax.broadcasted_iota(jnp.int32, imp.shape, 1)
    cur = past // _SEL_BLOCK
    score = jnp.where((j_i == 0) | (j_i == cur - 1), _FORCED, imp)
    sel = _topk_mask_rows(score.T, _SEL_TOPN - 1).T
    bias = jnp.where(sel > 0.5, 0.0, _NEG).astype(_BF16)
    bias = jnp.concatenate([bias[k * ts:(k + 1) * ts] for k in range(_NSA_KVH) for _ in range(_NSA_G)], axis=0)

    wb = kwc_ref.shape[1]
    s_w = lax.dot_general(q, kwc_ref[0].astype(_BF16), _NT, preferred_element_type=_F32)
    w_i = lax.broadcasted_iota(jnp.int32, (1, wb), 1)
    w_ok = (past - wb + w_i) > (tpos - _WINDOW)
    s_w = jnp.where(w_ok, s_w, _NEG)
    s_wn = jnp.where(new_ok, lax.dot_general(q, kwn_ref[0], _NT, preferred_element_type=_F32), _NEG)
    m_w = jnp.maximum(jnp.max(s_w, -1, keepdims=True), jnp.max(s_wn, -1, keepdims=True))
    p_w = jnp.where(w_ok, jnp.exp(s_w - m_w), 0.0)
    p_wn = jnp.where(new_ok, jnp.exp(s_wn - m_w), 0.0)
    l_w = jnp.sum(p_w, -1, keepdims=True) + jnp.sum(p_wn, -1, keepdims=True)
    o_w = (jnp.dot(p_w.astype(_BF16), vwc_ref[0].astype(_BF16), preferred_element_type=_F32)
           + jnp.dot(p_wn.astype(_BF16), vwn_ref[0], preferred_element_type=_F32)) / jnp.maximum(l_w, 1.0)

    for cp in copies:
        cp.wait()
    s_s = (lax.dot_general(q, ksel_scr[...].astype(_BF16), _NT, preferred_element_type=_F32)
           + jnp.dot(bias, e_ref[...], preferred_element_type=_F32))
    s_sn = jnp.where(new_ok, lax.dot_general(q, ksn_ref[0], _NT, preferred_element_type=_F32), _NEG)
    m_s = jnp.maximum(jnp.max(s_s, -1, keepdims=True), jnp.max(s_sn, -1, keepdims=True))
    p_s = jnp.exp(s_s - m_s)
    p_sn = jnp.where(new_ok, jnp.exp(s_sn - m_s), 0.0)
    l_s = jnp.sum(p_s, -1, keepdims=True) + jnp.sum(p_sn, -1, keepdims=True)
    o_s = (jnp.dot(p_s.astype(_BF16), vsel_scr[...].astype(_BF16), preferred_element_type=_F32)
           + jnp.dot(p_sn.astype(_BF16), vsn_ref[0], preferred_element_type=_F32)) / jnp.maximum(l_s, 1.0)

    g = g_ref[0]
    o_ref[0] = g[:, 0:1] * o_c + g[:, 1:2] * o_s + g[:, 2:3] * o_w


def _nsa_sample(y, kcb, vcb, pool_k, pool_v, win_k, win_v, page_table, b, ts):
    n_pages = page_table.shape[1]
    past = n_pages * _PAGE
    n_past = past // _SEL_BLOCK
    assert ts <= _SEL_BLOCK and _SEL_TOPN - 1 <= n_past <= 128
    rows = _NSA_KVH * _NSA_G * ts
    col = lambda c, w: y[:, c:c + w]
    q = _block_diag_queries(col(_C_NQ, _NSA_Q).reshape(b, ts, _NSA_H, _NSA_DH))
    g = _gate_rows(col(_C_NG, 3 * _NSA_H).reshape(b, ts, _NSA_KVH, _NSA_G, 3))
    new = lambda c: col(c, _NSA_KV).astype(_BF16).reshape(b, ts, _NSA_KV)
    expand = (jnp.arange(past)[None, :] // _SEL_BLOCK == jnp.arange(n_past)[:, None]).astype(_BF16)
    per_b = lambda a: pl.BlockSpec((1,) + a.shape[1:], lambda i, pt, nd=a.ndim: (i,) + (0,) * (nd - 1))
    blocked = [q, g, kcb.astype(_BF16), vcb.astype(_BF16), new(_C_KS), new(_C_VS),
               win_k.reshape(b, -1, _NSA_KV), win_v.reshape(b, -1, _NSA_KV), new(_C_KW), new(_C_VW)]
    pools = [pool_k.reshape(-1, _NSA_KV), pool_v.reshape(-1, _NSA_KV)]
    grid_spec = pltpu.PrefetchScalarGridSpec(
        num_scalar_prefetch=1,
        grid=(b,),
        in_specs=[per_b(a) for a in blocked] + [pl.BlockSpec(expand.shape, lambda i, pt: (0, 0))]
        + [pl.BlockSpec(memory_space=pl.ANY)] * 2,
        out_specs=pl.BlockSpec((1, rows, _NSA_KV), lambda i, pt: (i, 0, 0)),
        scratch_shapes=[pltpu.VMEM((past, _NSA_KV), _F32), pltpu.VMEM((past, _NSA_KV), _F32),
                        pltpu.SemaphoreType.DMA((2,))],
    )
    o = pl.pallas_call(
        functools.partial(_nsa_sample_body, past=past, ts=ts),
        grid_spec=grid_spec,
        out_shape=jax.ShapeDtypeStruct((b, rows, _NSA_KV), _F32),
        compiler_params=_cparams(("arbitrary",)),
        name="nsa_sample",
    )(page_table, *blocked, expand, *pools)
    return _own_head_rows(o, ts).reshape(b * ts, _NSA_Q)


def _merge_body(o_ref, srg_ref, on_ref, mg0_ref, mg1_ref, x_ref, g1_ref, wbr_ref, wbn_ref, wout_ref,
                lng_ref, lnb_ref, out_ref):
    o = o_ref[...]
    parts = []
    for h in range(_RET_H):
        seg = o[:, h * _RET_DV:(h + 1) * _RET_DV]
        mu = jnp.mean(seg, -1, keepdims=True)
        d = seg - mu
        var = jnp.mean(d * d, -1, keepdims=True)
        parts.append(d * lax.rsqrt(var + _LN_EPS))
    a = (jnp.concatenate(parts, axis=1) * srg_ref[...]).astype(_BF16)
    br = jnp.dot(a, wbr_ref[...], preferred_element_type=_F32)
    bn = jnp.dot(on_ref[...].astype(_BF16), wbn_ref[...], preferred_element_type=_F32)
    m = (mg0_ref[...] * br + mg1_ref[...] * bn).astype(_BF16)
    yv = jnp.dot(m, wout_ref[...], preferred_element_type=_F32)
    r = _ALPHA * x_ref[...] + g1_ref[0] * yv
    out_ref[...] = _layer_norm(r, lng_ref[...], lnb_ref[...])


def _merge(o_r, y, o_n, x2, g13, mod_map, w_br_ret, w_br_nsa, w_out, ln_g, ln_b, tm):
    n = x2.shape[0]
    r = g13.shape[1]
    row = lambda c: pl.BlockSpec((tm, _D), lambda i: (i, c))
    wbr, wbn, wout = w_br_ret.astype(_BF16), w_br_nsa.astype(_BF16), w_out.astype(_BF16)
    lng, lnb = ln_g.reshape(1, _D), ln_b.reshape(1, _D)
    return pl.pallas_call(
        _merge_body,
        grid=(n // tm,),
        in_specs=[row(0), row(_C_RG // _D), pl.BlockSpec((tm, _NSA_Q), lambda i: (i, 0)),
                  row(_C_MG // _D), row(_C_MG // _D + 1), row(0),
                  pl.BlockSpec((1, r, _D), lambda i: (mod_map(i), 0, 0))]
        + [_full_spec(a, 1) for a in (wbr, wbn, wout, lng, lnb)],
        out_specs=row(0),
        out_shape=jax.ShapeDtypeStruct((n, _D), _F32),
        compiler_params=_cparams(("parallel",)),
        name="merge_ln1",
    )(o_r, y, o_n, y, y, x2, g13, wbr, wbn, wout, lng, lnb)


def _topk_rows(work, k):
    r = work.shape[0]
    iota = lax.broadcasted_iota(jnp.int32, work.shape, 0)
    vals = []
    for _ in range(k):
        mx = jnp.max(work, axis=0, keepdims=True)
        idx = jnp.min(jnp.where(work == mx, iota, r), axis=0, keepdims=True)
        work = jnp.where(iota == idx, -jnp.inf, work)
        vals.append(mx)
    return vals


def _peer_select_body(x_ref, sc_ref, sh_ref, wq_ref, sk_ref, xb_ref, st_ref, thr_ref, mz_ref, sv_scr):
    xb = (x_ref[...] * (1.0 + sc_ref[0]) + sh_ref[0]).astype(_BF16)
    xb_ref[...] = xb
    q = jnp.dot(xb, wq_ref[...], preferred_element_type=_F32).astype(_BF16)
    k = _PEER_TOPK
    neg = jnp.full((1, q.shape[0]), -jnp.inf, _F32)
    for h in range(_PEER_H):
        for p in range(2):
            hp = 2 * h + p
            st = lax.dot_general(sk_ref[hp], q[:, hp * _PEER_HALF:(hp + 1) * _PEER_HALF], _NT,
                                 preferred_element_type=_F32)
            st_ref[hp] = st
            for r, v in enumerate(_topk_rows(st, k)):
                sv_scr[p, r:r + 1, :] = v
        sv0, sv1 = sv_scr[0], sv_scr[1]
        row8 = lax.broadcasted_iota(jnp.int32, (8, 1), 0)
        row16 = lax.broadcasted_iota(jnp.int32, (k, 1), 0)
        groups = [sv0[0:1] + sv1, jnp.where(row16 == 0, neg, sv0 + sv1[0:1])]
        for a in range(1, 8):
            groups.append(jnp.where(row8 == 0, neg, sv0[a:a + 1] + sv1[0:8]))
        top = _topk_rows(jnp.concatenate(groups, axis=0), k)
        m = top[0]
        z = functools.reduce(lambda u, v: u + v, [jnp.exp(t - m) for t in top])
        thr_ref[h:h + 1, :] = top[k - 1]
        mz_ref[h:h + 1, :] = m + jnp.log(z)


def _peer_select(x1, sc3, sh3, mod_map, wq, subkeys, tn):
    n = x1.shape[0]
    r = sc3.shape[1]
    hp = 2 * _PEER_H
    wqb = wq.astype(_BF16)
    sk = subkeys.astype(_BF16).reshape(hp, _PEER_NK, _PEER_HALF)
    mod = pl.BlockSpec((1, r, _D), lambda i: (mod_map(i), 0, 0))
    return pl.pallas_call(
        _peer_select_body,
        grid=(n // tn,),
        in_specs=[pl.BlockSpec((tn, _D), lambda i: (i, 0)), mod, mod, _full_spec(wqb, 1), _full_spec(sk, 1)],
        out_specs=[pl.BlockSpec((tn, _D), lambda i: (i, 0)),
                   pl.BlockSpec((hp, _PEER_NK, tn), lambda i: (0, 0, i)),
                   pl.BlockSpec((_PEER_H, tn), lambda i: (0, i)),
                   pl.BlockSpec((_PEER_H, tn), lambda i: (0, i))],
        out_shape=[jax.ShapeDtypeStruct((n, _D), _BF16),
                   jax.ShapeDtypeStruct((hp, _PEER_NK, n), _F32),
                   jax.ShapeDtypeStruct((_PEER_H, n), _F32),
                   jax.ShapeDtypeStruct((_PEER_H, n), _F32)],
        scratch_shapes=[pltpu.VMEM((2, _PEER_TOPK, tn), _F32)],
        compiler_params=_cparams(("parallel",)),
        name="peer_select",
    )(x1, sc3, sh3, wqb, sk)


def _peer_main_body(xb_ref, u_ref, v_ref, st_ref, thr_ref, mz_ref, x_ref, g2_ref, lng_ref, lnb_ref,
                    out_ref, acc_ref, e_scr, g_scr):
    e = pl.program_id(1)
    slabs = u_ref.shape[0] // _PEER_NK
    n = xb_ref.shape[0]

    @pl.when(e == 0)
    def _():
        acc_ref[...] = jnp.zeros_like(acc_ref)
        for h in range(_PEER_H):
            s1 = st_ref[2 * h + 1]
            m1 = jnp.max(s1, axis=0, keepdims=True)
            e_scr[2 * h] = jnp.exp(st_ref[2 * h] - (mz_ref[h:h + 1, :] - m1))
            e_scr[2 * h + 1] = jnp.exp(s1 - m1)

    for sl in range(slabs):
        i = e * slabs + sl
        gt = jnp.zeros((_PEER_NK, n), _F32)
        for h in range(_PEER_H):
            c = st_ref[2 * h, pl.ds(i, 1), :] + st_ref[2 * h + 1]
            gt = jnp.where(c >= thr_ref[h:h + 1, :], gt + e_scr[2 * h, pl.ds(i, 1), :] * e_scr[2 * h + 1], gt)
        g_scr[sl * _PEER_NK:(sl + 1) * _PEER_NK, :] = gt
    act = _gelu(lax.dot_general(u_ref[...], xb_ref[...], _NT, preferred_element_type=_F32))
    wt = (act * g_scr[...]).astype(_BF16)
    acc_ref[...] += lax.dot_general(wt, v_ref[...], _TN_DIMS, preferred_element_type=_F32)

    @pl.when(e == pl.num_programs(1) - 1)
    def _():
        r = _ALPHA * x_ref[...] + g2_ref[0] * acc_ref[...]
        out_ref[...] = _layer_norm(r, lng_ref[...], lnb_ref[...])


def _peer_main(xb, u_b, v_b, st, thr, mz, x1, g23, mod_map, ln_g, ln_b, tn, ec):
    n = x1.shape[0]
    r = g23.shape[1]
    ne = u_b.shape[0]
    hp = 2 * _PEER_H
    lng, lnb = ln_g.reshape(1, _D), ln_b.reshape(1, _D)
    return pl.pallas_call(
        _peer_main_body,
        grid=(n // tn, ne // ec),
        in_specs=[pl.BlockSpec((tn, _D), lambda i, e: (i, 0)),
                  pl.BlockSpec((ec, _D), lambda i, e: (e, 0)),
                  pl.BlockSpec((ec, _D), lambda i, e: (e, 0)),
                  pl.BlockSpec((hp, _PEER_NK, tn), lambda i, e: (0, 0, i)),
                  pl.BlockSpec((_PEER_H, tn), lambda i, e: (0, i)),
                  pl.BlockSpec((_PEER_H, tn), lambda i, e: (0, i)),
                  pl.BlockSpec((tn, _D), lambda i, e: (i, 0)),
                  pl.BlockSpec((1, r, _D), lambda i, e: (mod_map(i), 0, 0)),
                  _full_spec(lng, 2), _full_spec(lnb, 2)],
        out_specs=pl.BlockSpec((tn, _D), lambda i, e: (i, 0)),
        out_shape=jax.ShapeDtypeStruct((n, _D), _F32),
        scratch_shapes=[pltpu.VMEM((tn, _D), _F32), pltpu.VMEM((hp, _PEER_NK, tn), _F32),
                        pltpu.VMEM((ec, tn), _F32)],
        compiler_params=_cparams(("parallel", "arbitrary")),
        name="peer_main",
    )(xb, u_b, v_b, st, thr, mz, x1, g23, lng, lnb)


_PEER_SEL_TOKENS = 256
_PEER_MAIN_TOKENS = 512
_PEER_EXPERT_CHUNK = 1024


def _peer_and_ln(x1, sc3, sh3, sel_map, g3, main_map, wq, subkeys, u_b, v_b, ln_g, ln_b):
    n = x1.shape[0]
    xb, st, thr, mz = _peer_select(x1, sc3, sh3, sel_map, wq, subkeys, min(_PEER_SEL_TOKENS, n))
    return _peer_main(xb, u_b, v_b, st, thr, mz, x1, g3, main_map, ln_g, ln_b, min(_PEER_MAIN_TOKENS, n),
                      _PEER_EXPERT_CHUNK)


def kernel(x_prompt, x_sample, c_prompt, c_sample, state_ret, cache_cmp_k, cache_cmp_v, cache_sel_k, cache_sel_v,
           cache_win_k, cache_win_v, page_table, w_ada, b_ada, w_in, cmp_pe_k, cmp_w1_k, cmp_w2_k, cmp_pe_v,
           cmp_w1_v, cmp_w2_v, w_br_ret, w_br_nsa, w_out, ln1_g, ln1_b, peer_wq, peer_subkeys, peer_u, peer_v,
           ln2_g, ln2_b):
    l = 0
    B, T, _ = x_prompt.shape
    Bs, Ts, _ = x_sample.shape
    n_pages = page_table.shape[1]
    past = n_pages * _PAGE
    wcat = _build_wcat(w_in[l])
    u_b, v_b = peer_u[l].astype(_BF16), peer_v[l].astype(_BF16)
    cw_k = _compress_weights(cmp_pe_k[l], cmp_w1_k[l], cmp_w2_k[l])
    cw_v = _compress_weights(cmp_pe_v[l], cmp_w1_v[l], cmp_w2_v[l])
    stride_w = _CMP_STRIDE * _NSA_KV
    cache5 = lambda a, b, t: a.reshape(b, t, _NSA_KVH, _NSA_DH)

    ns = Bs * Ts
    tms = min(512, ns)
    ada_s = _linear(c_sample, w_ada[l], b_ada[l])
    rowmod = lambda a, tile: jnp.broadcast_to(a[:, None, :], (Bs, Ts, _D)).reshape(ns // tile, tile, _D)
    mods = jnp.split(ada_s, 6, axis=-1)
    xs2 = x_sample.reshape(ns, _D)
    pos = past + jnp.arange(Ts, dtype=jnp.int32)
    tabs_s = _rope_tables(jnp.tile(pos, tms // Ts))
    ident = lambda i: i
    ys = _inproj(xs2, rowmod(mods[1], tms), rowmod(mods[0], tms), ident, wcat, tabs_s, lambda i: 0, tms)
    cols = lambda c, w: ys[:, c:c + w]
    assert Ts < _CMP_STRIDE
    kcb_s = _compress_paged(cache_cmp_k[l].reshape(-1, stride_w), page_table, cw_k).reshape(Bs, -1, _NSA_KV)
    vcb_s = _compress_paged(cache_cmp_v[l].reshape(-1, stride_w), page_table, cw_v).reshape(Bs, -1, _NSA_KV)
    o_ns = _nsa_sample(ys, kcb_s, vcb_s, cache_sel_k[l], cache_sel_v[l], cache_win_k[l], cache_win_v[l],
                       page_table, Bs, Ts)
    o_rs, r_new = _retention_sample(ys, state_ret[l], Bs, Ts)
    tms2 = min(256, ns)
    x1s = _merge(o_rs, ys, o_ns, xs2, rowmod(mods[2], tms2), ident, w_br_ret[l], w_br_nsa[l], w_out[l],
                 ln1_g[l], ln1_b[l], tms2)
    sel_t, main_t = min(_PEER_SEL_TOKENS, ns), min(_PEER_MAIN_TOKENS, ns)
    xs_out = _peer_and_ln(x1s, rowmod(mods[4], sel_t), rowmod(mods[3], sel_t), ident, rowmod(mods[5], main_t), ident,
                          peer_wq[l], peer_subkeys[l], u_b, v_b, ln2_g[l], ln2_b[l])
    new5 = lambda c: cache5(cols(c, _NSA_KV), Bs, Ts)
    kwin = jnp.concatenate([cache_win_k[l], new5(_C_KW)], axis=1)
    vwin = jnp.concatenate([cache_win_v[l], new5(_C_VW)], axis=1)
    sample_state = (r_new, new5(_C_KC), new5(_C_VC), new5(_C_KS), new5(_C_VS), kwin[:, Ts:], vwin[:, Ts:])

    tm = 512
    tpb = T // tm
    ada_p = _linear(c_prompt, w_ada[l], b_ada[l])
    sh1, sc1, g1, sh2, sc2, g2 = [a.reshape(B, 1, _D) for a in jnp.split(ada_p, 6, axis=-1)]
    xp2 = x_prompt.reshape(B * T, _D)
    tabs_p = _rope_tables(jnp.arange(T, dtype=jnp.int32))
    y = _inproj(xp2, sc1, sh1, lambda i: i // tpb, wcat, tabs_p, lambda i: i % tpb, tm)
    col = lambda c, w: y[:, c:c + w]
    o_r, R_p = _retention_prompt(y, B, T)
    nc = T // _CMP_STRIDE
    kcb = _compress_prompt(col(_C_KC, _NSA_KV).reshape(B * nc, stride_w), B, cw_k).reshape(B, nc, _NSA_KV)
    vcb = _compress_prompt(col(_C_VC, _NSA_KV).reshape(B * nc, stride_w), B, cw_v).reshape(B, nc, _NSA_KV)
    o_n = _nsa_prompt(y, kcb, vcb, B, T)
    tm2 = 256
    tpb2 = T // tm2
    bmap2 = lambda i: i // tpb2
    x1 = _merge(o_r, y, o_n, xp2, g1, bmap2, w_br_ret[l], w_br_nsa[l], w_out[l], ln1_g[l], ln1_b[l], tm2)
    xp_out = _peer_and_ln(x1, sc2, sh2, lambda i: i // (T // _PEER_SEL_TOKENS), g2,
                          lambda i: i // (T // _PEER_MAIN_TOKENS), peer_wq[l], peer_subkeys[l], u_b, v_b,
                          ln2_g[l], ln2_b[l])
    wb = min(_WINDOW, T)
    seq5 = lambda c: cache5(col(c, _NSA_KV), B, T)
    prompt_state = (R_p, seq5(_C_KC), seq5(_C_VC), seq5(_C_KS), seq5(_C_VS),
                    seq5(_C_KW)[:, T - wb:], seq5(_C_VW)[:, T - wb:])

    outs =[xp_out.reshape(B, T, _D), xs_out.reshape(Bs, Ts, _D)]
    outs += [a[None] for a in prompt_state]
    outs += [a[None] for a in sample_state]
    return tuple(outs)
```

```python
import functools
import math

import jax
import jax.numpy as jnp
import numpy as np
from jax import lax
from jax.experimental import pallas as pl
from jax.experimental.pallas import tpu as pltpu

_F32 = jnp.float32
_BF16 = jnp.bfloat16

_D = 1024
_RET_H, _RET_DK, _RET_DV, _RET_C = 4, 128, 256, 128
_NSA_H, _NSA_KVH, _NSA_G, _NSA_DH = 8, 2, 4, 64
_CMP_BLOCK, _CMP_STRIDE, _CMP_HID = 32, 16, 128
_SEL_BLOCK, _SEL_TOPN, _WINDOW, _QB = 64, 16, 512, 64
_FORCED = 1.0e4
_PEER_H, _PEER_NK, _PEER_HALF, _PEER_TOPK = 8, 128, 128, 16
_PAGE = 128
_ALPHA = 2.0 ** 0.25
_LN_EPS = 1e-5
_NEG = -1e30
_ROPE_THETA = 10000.0
_RET_QK = _RET_H * _RET_DK
_RET_V = _RET_H * _RET_DV
_NSA_Q = _NSA_H * _NSA_DH
_NSA_KV = _NSA_KVH * _NSA_DH
_IN_SECTIONS = (_RET_QK, _RET_QK, _RET_V, _RET_V, _NSA_Q, 6 * _NSA_KV, 3 * _NSA_H, 2 * _D)
_IN_SPLITS = tuple(int(s) for s in np.cumsum(_IN_SECTIONS)[:-1])
_RET_LOG_G = tuple(math.log1p(-2.0 ** (-5.0 - h)) for h in range(_RET_H))

_VMEM_LIMIT = 48 * 1024 * 1024

_TN = 512
_C_RQ, _C_RK, _C_RV, _C_RG, _C_MG, _C_NQ = 0, 512, 1024, 2048, 3072, 5120
_C_KC, _C_KS, _C_KW = 5632, 5760, 5888
_C_VC, _C_VS, _C_VW, _C_NG = 6144, 6272, 6400, 6528
_IN_COLS = 13 * _TN

_SEL_CHUNK = 512
_NSA_ROWS = _NSA_KVH * _NSA_G * _QB

_NT = (((1,), (1,)), ((), ()))
_TN_DIMS = (((0,), (0,)), ((), ()))


def _cparams(sem):
    return pltpu.CompilerParams(dimension_semantics=sem, vmem_limit_bytes=_VMEM_LIMIT)


def _gelu(x):
    return 0.5 * x * (1.0 + lax.erf(x * (1.0 / math.sqrt(2.0))))


def _layer_norm(r, g, b):
    mu = jnp.mean(r, -1, keepdims=True)
    d = r - mu
    var = jnp.mean(d * d, -1, keepdims=True)
    return d * lax.rsqrt(var + _LN_EPS) * g + b


def _masked_softmax(s, mask):
    s = jnp.where(mask, s, _NEG)
    p = jnp.where(mask, jnp.exp(s - jnp.max(s, -1, keepdims=True)), 0.0)
    return p / jnp.maximum(jnp.sum(p, -1, keepdims=True), 1.0)


def _full_spec(a, grid_rank):
    zeros = (0,) * a.ndim
    return pl.BlockSpec(a.shape, {1: lambda i: zeros, 2: lambda i, j: zeros}[grid_rank])


def _linear_body(a_ref, w_ref, b_ref, o_ref):
    o_ref[...] = jnp.dot(a_ref[...].astype(_BF16), w_ref[...], preferred_element_type=_F32) + b_ref[...]


def _linear(a, w, b, tn=512):
    m, k = a.shape
    n = w.shape[1]
    return pl.pallas_call(
        _linear_body,
        grid=(n // tn,),
        in_specs=[pl.BlockSpec((m, k), lambda j: (0, 0)),
                  pl.BlockSpec((k, tn), lambda j: (0, j)),
                  pl.BlockSpec((1, tn), lambda j: (0, j))],
        out_specs=pl.BlockSpec((m, tn), lambda j: (0, j)),
        out_shape=jax.ShapeDtypeStruct((m, n), _F32),
        compiler_params=_cparams(("arbitrary",)),
        name="ada_linear",
    )(a, w.astype(_BF16), b.reshape(1, n))


def _build_wcat(w_in):
    rq, rk, rv, rg, nq, nkv, ng, mg = jnp.split(w_in, _IN_SPLITS, axis=1)
    nkv6 = nkv.reshape(_D, 6, _NSA_KV)
    kc, vc, ks, vs, kw, vw = [nkv6[:, i] for i in range(6)]
    z = jnp.zeros((_D, _NSA_KV), w_in.dtype)
    ngp = jnp.pad(ng, ((0, 0), (0, _NSA_KV - ng.shape[1])))
    wcat = jnp.concatenate([rq, rk, rv, rg, mg, nq, kc, ks, kw, z, vc, vs, vw, ngp], axis=1)
    return wcat.astype(_BF16)


def _rope_tables(pos):
    out = []
    for half in (_RET_DK // 2, _NSA_DH // 2):
        inv = _ROPE_THETA ** (-jnp.arange(half, dtype=_F32) / half)
        ang = pos.astype(_F32)[:, None] * inv[None, :]
        cos, sin = jnp.cos(ang), jnp.sin(ang)
        reps = _TN // (2 * half)
        out.append(jnp.tile(jnp.concatenate([cos, cos], -1), (1, reps)))
        out.append(jnp.tile(jnp.concatenate([-sin, sin], -1), (1, reps)))
    return out


def _inproj_body(x_ref, sc_ref, sh_ref, w_ref, c128_ref, s128_ref, c64_ref, s64_ref, y_ref, a_scr):
    j = pl.program_id(1)

    @pl.when(j == 0)
    def _():
        a_scr[...] = (x_ref[...] * (1.0 + sc_ref[0]) + sh_ref[0]).astype(_BF16)

    y = jnp.dot(a_scr[...], w_ref[...], preferred_element_type=_F32)
    width = y.shape[1]
    lane = lax.broadcasted_iota(jnp.int32, y.shape, 1)

    def rope(half, c_ref, s_ref):
        first = (lane % (2 * half)) < half
        partner = jnp.where(first, pltpu.roll(y, width - half, 1), pltpu.roll(y, half, 1))
        return y * c_ref[...] + partner * s_ref[...]

    @pl.when(j == _C_RQ // _TN)
    def _():
        y_ref[...] = rope(_RET_DK // 2, c128_ref, s128_ref)

    @pl.when(j == _C_RK // _TN)
    def _():
        y_ref[...] = rope(_RET_DK // 2, c128_ref, s128_ref) * (_RET_DK ** -0.5)

    @pl.when((j >= _C_RV // _TN) & (j < _C_RG // _TN))
    def _():
        y_ref[...] = y

    @pl.when((j >= _C_RG // _TN) & (j < _C_MG // _TN))
    def _():
        y_ref[...] = y * jax.nn.sigmoid(y)

    @pl.when((j >= _C_MG // _TN) & (j < _C_NQ // _TN))
    def _():
        y_ref[...] = jax.nn.sigmoid(y)

    @pl.when((j >= _C_NQ // _TN) & (j < _C_VC // _TN))
    def _():
        y_ref[...] = rope(_NSA_DH // 2, c64_ref, s64_ref)

    @pl.when(j == _C_VC // _TN)
    def _():
        y_ref[...] = jnp.where(lane < _C_NG - _C_VC, y, jax.nn.sigmoid(y))


def _inproj(x2, sc3, sh3, mod_map, wcat, tabs, tab_map, tm):
    n = x2.shape[0]
    r = sc3.shape[1]
    tab_spec = pl.BlockSpec((tm, _TN), lambda i, j: (tab_map(i), 0))
    return pl.pallas_call(
        _inproj_body,
        grid=(n // tm, _IN_COLS // _TN),
        in_specs=[pl.BlockSpec((tm, _D), lambda i, j: (i, 0)),
                  pl.BlockSpec((1, r, _D), lambda i, j: (mod_map(i), 0, 0)),
                  pl.BlockSpec((1, r, _D), lambda i, j: (mod_map(i), 0, 0)),
                  pl.BlockSpec((_D, _TN), lambda i, j: (0, j)),
                  tab_spec, tab_spec, tab_spec, tab_spec],
        out_specs=pl.BlockSpec((tm, _TN), lambda i, j: (i, j)),
        out_shape=jax.ShapeDtypeStruct((n, _IN_COLS), _F32),
        scratch_shapes=[pltpu.VMEM((tm, _D), _BF16)],
        compiler_params=_cparams(("parallel", "arbitrary")),
        name="in_proj",
    )(x2, sc3, sh3, wcat, *tabs)


def _ret_consts(c):
    lg = jnp.asarray(_RET_LOG_G, _F32)[:, None, None]
    n = jnp.arange(c, dtype=_F32)
    diff = n[:, None] - n[None, :]
    decay = jnp.where(diff >= 0, jnp.exp(lg * jnp.maximum(diff, 0.0)), 0.0)
    cs = jnp.broadcast_to(jnp.exp(lg * (n[:, None] + 1.0)), (_RET_H, c, _RET_DV))
    kd = jnp.broadcast_to(jnp.exp(lg * (c - 1.0 - n[:, None])), (_RET_H, c, _RET_DK))
    return decay, cs, kd


def _ret_step(q_ref, k_ref, v_ref, dec_ref, cs_ref, kd_ref, o_ref, state, chunk):
    new = []
    for h in range(_RET_H):
        q = q_ref[:, h * _RET_DK:(h + 1) * _RET_DK].astype(_BF16)
        k = k_ref[:, h * _RET_DK:(h + 1) * _RET_DK]
        v = v_ref[:, h * _RET_DV:(h + 1) * _RET_DV].astype(_BF16)
        r = state(h)
        s = lax.dot_general(q, k.astype(_BF16), _NT, preferred_element_type=_F32) * dec_ref[h]
        inner = jnp.dot(s.astype(_BF16), v, preferred_element_type=_F32)
        cross = jnp.dot(q, r.astype(_BF16), preferred_element_type=_F32) * cs_ref[h]
        o_ref[:, h * _RET_DV:(h + 1) * _RET_DV] = inner + cross
        k_dec = (k * kd_ref[h]).astype(_BF16)
        new.append(r * math.exp(_RET_LOG_G[h] * chunk)
                   + lax.dot_general(k_dec, v, _TN_DIMS, preferred_element_type=_F32))
    return new


def _ret_prompt_body(q_ref, k_ref, v_ref, dec_ref, cs_ref, kd_ref, o_ref, r_ref):
    @pl.when(pl.program_id(1) == 0)
    def _():
        r_ref[...] = jnp.zeros_like(r_ref)

    new = _ret_step(q_ref, k_ref, v_ref, dec_ref, cs_ref, kd_ref, o_ref, lambda h: r_ref[0, h], _RET_C)
    for h in range(_RET_H):
        r_ref[0, h] = new[h]


def _retention_prompt(y, b, t):
    c = _RET_C
    nc = t // c
    consts = _ret_consts(c)
    return pl.pallas_call(
        _ret_prompt_body,
        grid=(b, nc),
        in_specs=[pl.BlockSpec((c, _RET_QK), lambda i, j: (i * nc + j, _C_RQ // _RET_QK)),
                  pl.BlockSpec((c, _RET_QK), lambda i, j: (i * nc + j, _C_RK // _RET_QK)),
                  pl.BlockSpec((c, _RET_V), lambda i, j: (i * nc + j, _C_RV // _RET_V))]
        + [_full_spec(a, 2) for a in consts],
        out_specs=[pl.BlockSpec((c, _RET_V), lambda i, j: (i * nc + j, 0)),
                   pl.BlockSpec((1, _RET_H, _RET_DK, _RET_DV), lambda i, j: (i, 0, 0, 0))],
        out_shape=[jax.ShapeDtypeStruct((b * t, _RET_V), _F32),
                   jax.ShapeDtypeStruct((b, _RET_H, _RET_DK, _RET_DV), _F32)],
        compiler_params=_cparams(("parallel", "arbitrary")),
        name="retention_prompt",
    )(y, y, y, *consts)


def _ret_sample_body(q_ref, k_ref, v_ref, st_ref, dec_ref, cs_ref, kd_ref, o_ref, r_ref, *, chunk):
    new = _ret_step(q_ref, k_ref, v_ref, dec_ref, cs_ref, kd_ref, o_ref, lambda h: st_ref[0, h], chunk)
    for h in range(_RET_H):
        r_ref[0, h] = new[h]


def _retention_sample(y, state, b, t):
    consts = _ret_consts(t)
    st_spec = pl.BlockSpec((1, _RET_H, _RET_DK, _RET_DV), lambda i: (i, 0, 0, 0))
    return pl.pallas_call(
        functools.partial(_ret_sample_body, chunk=t),
        grid=(b,),
        in_specs=[pl.BlockSpec((t, _RET_QK), lambda i: (i, _C_RQ // _RET_QK)),
                  pl.BlockSpec((t, _RET_QK), lambda i: (i, _C_RK // _RET_QK)),
                  pl.BlockSpec((t, _RET_V), lambda i: (i, _C_RV // _RET_V)),
                  st_spec] + [_full_spec(a, 1) for a in consts],
        out_specs=[pl.BlockSpec((t, _RET_V), lambda i: (i, 0)), st_spec],
        out_shape=[jax.ShapeDtypeStruct((b * t, _RET_V), _F32),
                   jax.ShapeDtypeStruct(state.shape, _F32)],
        compiler_params=_cparams(("parallel",)),
        name="retention_sample",
    )(y, y, y, state, *consts)


def _compress_weights(pe, w1, w2):
    w = w1.reshape(2, _CMP_STRIDE, _NSA_DH, _CMP_HID)
    eye = jnp.eye(_NSA_KVH, dtype=w1.dtype)
    wf = jnp.einsum('psde,kl->skdlpe', w, eye).reshape(_CMP_STRIDE * _NSA_KV, _NSA_KVH * 2 * _CMP_HID)
    pe8 = jnp.pad(pe.reshape(1, -1), ((0, 7), (0, 0)))
    return wf.astype(_BF16), pe8.astype(_BF16), w1.astype(_BF16), w2.astype(_BF16)


def _compress_compute(rows, wf_ref, pe_ref, w1_ref, w2_ref, o_ref, z_scr):
    ns = rows.shape[0]
    z = jnp.dot(rows.astype(_BF16), wf_ref[...], preferred_element_type=_F32)
    z_scr[pl.ds(0, ns), :] = z
    z_scr[pl.ds(ns, 8), :] = jnp.zeros((8, z.shape[1]), _F32)
    bias = jnp.dot(pe_ref[...], w1_ref[...], preferred_element_type=_F32)[0:1]
    for k in range(_NSA_KVH):
        c0 = k * 2 * _CMP_HID
        z_next = z_scr[pl.ds(1, ns), c0 + _CMP_HID:c0 + 2 * _CMP_HID]
        hid = _gelu(z[:, c0:c0 + _CMP_HID] + z_next + bias)
        o_ref[:, k * _NSA_DH:(k + 1) * _NSA_DH] = jnp.dot(hid.astype(_BF16), w2_ref[...],
                                                         preferred_element_type=_F32)


def _compress_body(rows_ref, wf_ref, pe_ref, w1_ref, w2_ref, o_ref, z_scr):
    _compress_compute(rows_ref[...], wf_ref, pe_ref, w1_ref, w2_ref, o_ref, z_scr)


def _compress_prompt(rows2, b, weights):
    ns = rows2.shape[0] // b
    return pl.pallas_call(
        _compress_body,
        grid=(b,),
        in_specs=[pl.BlockSpec((ns, rows2.shape[1]), lambda i: (i, 0))] + [_full_spec(a, 1) for a in weights],
        out_specs=pl.BlockSpec((ns, _NSA_KV), lambda i: (i, 0)),
        out_shape=jax.ShapeDtypeStruct((b * ns, _NSA_KV), _F32),
        scratch_shapes=[pltpu.VMEM((ns + 8, _NSA_KVH * 2 * _CMP_HID), _F32)],
        compiler_params=_cparams(("parallel",)),
        name="compress_prompt",
    )(rows2, *weights)


def _pool_token_minor(pool):
    return jnp.transpose(pool, (0, 2, 3, 1))


def _page_copies(pt_ref, b, pool_ref, dst_ref, sem):
    n_pages = pt_ref.shape[1]
    return [pltpu.make_async_copy(pool_ref.at[pt_ref[b, p]], dst_ref.at[:, :, pl.ds(p * _PAGE, _PAGE)], sem)
            for p in range(n_pages)]


def _compress_paged_body(pt_ref, pool_ref, wf_ref, pe_ref, w1_ref, w2_ref, o_ref, xt_scr, x_scr, rows_scr, z_scr,
                         sem):
    copies = _page_copies(pt_ref, pl.program_id(0), pool_ref, xt_scr, sem)
    for cp in copies:
        cp.start()
    for cp in copies:
        cp.wait()
    past = xt_scr.shape[2]
    ns = past // _CMP_STRIDE
    x_scr[...] = xt_scr[...].reshape(_NSA_KV, past).T
    for s in range(_CMP_STRIDE):
        rows_scr[:, s * _NSA_KV:(s + 1) * _NSA_KV] = x_scr[pl.ds(s, ns, stride=_CMP_STRIDE), :]
    _compress_compute(rows_scr[...], wf_ref, pe_ref, w1_ref, w2_ref, o_ref, z_scr)


def _compress_paged(pool_t, page_table, weights):
    b, n_pages = page_table.shape
    past = n_pages * _PAGE
    ns = past // _CMP_STRIDE
    grid_spec = pltpu.PrefetchScalarGridSpec(
        num_scalar_prefetch=1,
        grid=(b,),
        in_specs=[pl.BlockSpec(memory_space=pl.ANY)]
        + [pl.BlockSpec(a.shape, lambda i, pt, nd=a.ndim: (0,) * nd) for a in weights],
        out_specs=pl.BlockSpec((ns, _NSA_KV), lambda i, pt: (i, 0)),
        scratch_shapes=[pltpu.VMEM((_NSA_KVH, _NSA_DH, past), _F32),
                        pltpu.VMEM((past, _NSA_KV), _F32),
                        pltpu.VMEM((ns, _CMP_STRIDE * _NSA_KV), _F32),
                        pltpu.VMEM((ns + 8, _NSA_KVH * 2 * _CMP_HID), _F32),
                        pltpu.SemaphoreType.DMA],
    )
    return pl.pallas_call(
        _compress_paged_body,
        grid_spec=grid_spec,
        out_shape=jax.ShapeDtypeStruct((b * ns, _NSA_KV), _F32),
        compiler_params=_cparams(("arbitrary",)),
        name="compress_paged",
    )(page_table, pool_t, *weights)


def _overlap_matrix(n_cmp, n_sel_pad):
    nn = lax.broadcasted_iota(jnp.int32, (n_cmp, n_sel_pad), 0)
    jj = lax.broadcasted_iota(jnp.int32, (n_cmp, n_sel_pad), 1)
    per = _SEL_BLOCK // _CMP_STRIDE
    return ((nn < (jj + 1) * per) & (nn * _CMP_STRIDE + _CMP_BLOCK > jj * _SEL_BLOCK)).astype(_BF16)


def _importance(pcs, ov):
    hi = pcs.astype(_BF16)
    lo = (pcs - hi.astype(_F32)).astype(_BF16)
    return jnp.dot(hi, ov, preferred_element_type=_F32) + jnp.dot(lo, ov, preferred_element_type=_F32)


def _topk_mask_rows(work, k):
    r = work.shape[0]
    iota = lax.broadcasted_iota(jnp.int32, work.shape, 0)
    sel = jnp.zeros(work.shape, _F32)
    for _ in range(k):
        mx = jnp.max(work, axis=0, keepdims=True)
        idx = jnp.min(jnp.where(work == mx, iota, r), axis=0, keepdims=True)
        pick = iota == idx
        sel = jnp.where(pick, 1.0, sel)
        work = jnp.where(pick, -jnp.inf, work)
    return sel


def _nsa_prompt_body(q_ref, g_ref, kc_ref, vc_ref, ks_ref, vs_ref, kw_ref, vw_ref, o_ref, *, seq):
    qb = pl.program_id(1)
    rows = _NSA_ROWS
    n_sel_pad = ks_ref.shape[3] - _NSA_KV
    q = q_ref[0, 0]
    row = lax.broadcasted_iota(jnp.int32, (rows, 1), 0)
    tpos = qb * _QB + row % _QB

    n_cmp = kc_ref.shape[1]
    s_c = lax.dot_general(q, kc_ref[0], _NT, preferred_element_type=_F32)
    n_i = lax.broadcasted_iota(jnp.int32, (1, n_cmp), 1)
    p_c = _masked_softmax(s_c, (n_i * _CMP_STRIDE + _CMP_BLOCK - 1) <= tpos)
    o_c = jnp.dot(p_c.astype(_BF16), vc_ref[0], preferred_element_type=_F32)

    gq = _NSA_G * _QB
    pcs = jnp.concatenate(
        [functools.reduce(lambda u, v: u + v, [p_c[k * gq + g * _QB:k * gq + (g + 1) * _QB] for g in range(_NSA_G)])
         for k in range(_NSA_KVH)], axis=0)
    imp = _importance(pcs, _overlap_matrix(n_cmp, n_sel_pad))
    j_i = lax.broadcasted_iota(jnp.int32, imp.shape, 1)
    forced = (j_i == 0) | (j_i == qb) | (j_i == qb - 1)
    score = jnp.where(j_i <= qb, jnp.where(forced, _FORCED, imp), -1.0)
    score_t = score.T
    sel = (_topk_mask_rows(score_t, _SEL_TOPN) * (score_t >= 0.0).astype(_F32)).T
    bias = jnp.where(sel > 0.5, 0.0, _NEG).astype(_BF16)
    bias = jnp.concatenate([bias[k * _QB:(k + 1) * _QB] for k in range(_NSA_KVH) for _ in range(_NSA_G)], axis=0)

    q_sel = jnp.concatenate([q, bias], axis=1)
    k_i = lax.broadcasted_iota(jnp.int32, (1, _SEL_CHUNK), 1)

    def chunk_step(c, carry, causal):
        m, l, acc = carry
        s = lax.dot_general(q_sel, ks_ref[0, c], _NT, preferred_element_type=_F32)
        if causal:
            s = jnp.where((c * _SEL_CHUNK + k_i) <= tpos, s, _NEG)
        m_new = jnp.maximum(m, jnp.max(s, -1, keepdims=True))
        p = jnp.exp(s - m_new)
        alpha = jnp.exp(m - m_new)
        l = alpha * l + jnp.sum(p, -1, keepdims=True)
        acc = alpha * acc + jnp.dot(p.astype(_BF16), vs_ref[0, c], preferred_element_type=_F32)
        return m_new, l, acc

    init = (jnp.full((rows, 1), _NEG, _F32), jnp.zeros((rows, 1), _F32), jnp.zeros((rows, _NSA_KV), _F32))
    last = qb // (_SEL_CHUNK // _SEL_BLOCK)
    carry = lax.fori_loop(0, last, lambda c, cr: chunk_step(c, cr, False), init)
    _, l_s, acc_s = chunk_step(last, carry, True)
    o_s = acc_s / jnp.maximum(l_s, 1.0)

    wlen = _WINDOW + 2 * _QB
    w0 = pl.multiple_of(jnp.clip(qb * _QB - (_WINDOW + _QB), 0, seq - wlen), _QB)
    s_w = lax.dot_general(q, kw_ref[0, pl.ds(w0, wlen), :], _NT, preferred_element_type=_F32)
    kpos = w0 + lax.broadcasted_iota(jnp.int32, (1, wlen), 1)
    p_w = _masked_softmax(s_w, (kpos <= tpos) & (kpos > tpos - _WINDOW))
    o_w = jnp.dot(p_w.astype(_BF16), vw_ref[0, pl.ds(w0, wlen), :], preferred_element_type=_F32)

    g = g_ref[0, 0]
    o_ref[0, 0] = g[:, 0:1] * o_c + g[:, 1:2] * o_s + g[:, 2:3] * o_w


def _block_diag_queries(nq):
    *lead, t, _, _ = nq.shape
    q = (nq * (_NSA_DH ** -0.5)).astype(_BF16).reshape(*lead, t, _NSA_KVH, _NSA_G, _NSA_DH)
    q = jnp.moveaxis(q, -4, -2)
    eye = jnp.eye(_NSA_KVH, dtype=_BF16)
    q = q[..., None, :] * eye[:, None, None, :, None]
    return q.reshape(*lead, _NSA_KVH * _NSA_G * t, _NSA_KV)


def _gate_rows(ng):
    *lead, t, _, _, _ = ng.shape
    return jnp.moveaxis(ng, -4, -2).reshape(*lead, _NSA_KVH * _NSA_G * t, 3)


def _own_head_rows(o, t):
    lead = o.shape[:-2]
    o = o.reshape(*lead, _NSA_KVH, _NSA_G, t, _NSA_KVH, _NSA_DH)
    o = jnp.stack([o[..., k, :, :, k, :] for k in range(_NSA_KVH)], axis=-4)
    return jnp.moveaxis(o, -2, -4).reshape(*lead, t, _NSA_Q)


def _nsa_prompt(y, kcb, vcb, b, t):
    nqb = t // _QB
    n_sel_pad = 128
    assert nqb <= n_sel_pad and t % _SEL_CHUNK == 0
    col = lambda c, w: y[:, c:c + w]
    q = _block_diag_queries(col(_C_NQ, _NSA_Q).reshape(b, nqb, _QB, _NSA_H, _NSA_DH))
    g = _gate_rows(col(_C_NG, 3 * _NSA_H).reshape(b, nqb, _QB, _NSA_KVH, _NSA_G, 3))
    seq_b = lambda c: col(c, _NSA_KV).astype(_BF16).reshape(b, t, _NSA_KV)
    chunked = lambda a: a.reshape(b, t // _SEL_CHUNK, _SEL_CHUNK, a.shape[-1])
    block_id = (jnp.arange(t)[:, None] // _SEL_BLOCK == jnp.arange(n_sel_pad)[None, :]).astype(_BF16)
    ks = chunked(jnp.concatenate([seq_b(_C_KS), jnp.broadcast_to(block_id, (b, t, n_sel_pad))], axis=-1))
    vs, kw, vw = chunked(seq_b(_C_VS)), seq_b(_C_KW), seq_b(_C_VW)
    kc, vc = kcb.astype(_BF16), vcb.astype(_BF16)
    per_q = lambda last: pl.BlockSpec((1, 1, _NSA_ROWS, last), lambda i, j: (i, j, 0, 0))
    per_b = lambda a: pl.BlockSpec((1,) + a.shape[1:], lambda i, j: (i,) + (0,) * (a.ndim - 1))
    o = pl.pallas_call(
        functools.partial(_nsa_prompt_body, seq=t),
        grid=(b, nqb),
        in_specs=[per_q(_NSA_KV), per_q(3), per_b(kc), per_b(vc), per_b(ks), per_b(vs), per_b(kw), per_b(vw)],
        out_specs=per_q(_NSA_KV),
        out_shape=jax.ShapeDtypeStruct((b, nqb, _NSA_ROWS, _NSA_KV), _F32),
        compiler_params=_cparams(("parallel", "arbitrary")),
        name="nsa_prompt",
    )(q, g, kc, vc, ks, vs, kw, vw)
    return _own_head_rows(o, _QB).reshape(b * t, _NSA_Q)


def _nsa_sample_body(pt_ref, q_ref, g_ref, kc_ref, vc_ref, ksn_ref, vsn_ref, kwc_ref, vwc_ref, kwn_ref, vwn_ref,
                     e_ref, poolk_ref, poolv_ref, o_ref, ksel_scr, vsel_scr, sem, *, past, ts):
    b = pl.program_id(0)
    copies = (_page_copies(pt_ref, b, poolk_ref, ksel_scr, sem.at[0])
              + _page_copies(pt_ref, b, poolv_ref, vsel_scr, sem.at[1]))
    for cp in copies:
        cp.start()

    rows = _NSA_KVH * _NSA_G * ts
    q = q_ref[0]
    row = lax.broadcasted_iota(jnp.int32, (rows, 1), 0)
    t_new = row % ts
    tpos = past + t_new
    new_ok = lax.broadcasted_iota(jnp.int32, (1, ts), 1) <= t_new

    n_cmp = kc_ref.shape[1]
    s_c = lax.dot_general(q, kc_ref[0], _NT, preferred_element_type=_F32)
    n_i = lax.broadcasted_iota(jnp.int32, (1, n_cmp), 1)
    p_c = _masked_softmax(s_c, (n_i * _CMP_STRIDE + _CMP_BLOCK - 1) <= tpos)
    o_c = jnp.dot(p_c.astype(_BF16), vc_ref[0], preferred_element_type=_F32)

    n_past = e_ref.shape[0]
    gt = _NSA_G * ts
    pcs = jnp.concatenate(
        [functools.reduce(lambda u, v: u + v, [p_c[k * gt + g * ts:k * gt + (g + 1) * ts] for g in range(_NSA_G)])
         for k in range(_NSA_KVH)], axis=0)
    imp = _importance(pcs, _overlap_matrix(n_cmp, n_past))
    j_i = lax.broadcasted_iota(jnp.int32, imp.shape, 1)
    cur = past // _SEL_BLOCK
    score = jnp.where((j_i == 0) | (j_i == cur - 1), _FORCED, imp)
    sel = _topk_mask_rows(score.T, _SEL_TOPN - 1).T
    bias = jnp.where(sel > 0.5, 0.0, _NEG).astype(_BF16)
    bias = jnp.concatenate([bias[k * ts:(k + 1) * ts] for k in range(_NSA_KVH) for _ in range(_NSA_G)], axis=0)

    wb = kwc_ref.shape[2]
    s_w = jnp.dot(q, kwc_ref[0].astype(_BF16), preferred_element_type=_F32)
    w_i = lax.broadcasted_iota(jnp.int32, (1, wb), 1)
    w_ok = (past - wb + w_i) > (tpos - _WINDOW)
    s_w = jnp.where(w_ok, s_w, _NEG)
    s_wn = jnp.where(new_ok, lax.dot_general(q, kwn_ref[0], _NT, preferred_element_type=_F32), _NEG)
    m_w = jnp.maximum(jnp.max(s_w, -1, keepdims=True), jnp.max(s_wn, -1, keepdims=True))
    p_w = jnp.where(w_ok, jnp.exp(s_w - m_w), 0.0)
    p_wn = jnp.where(new_ok, jnp.exp(s_wn - m_w), 0.0)
    l_w = jnp.sum(p_w, -1, keepdims=True) + jnp.sum(p_wn, -1, keepdims=True)
    o_w = (lax.dot_general(p_w.astype(_BF16), vwc_ref[0].astype(_BF16), _NT, preferred_element_type=_F32)
           + jnp.dot(p_wn.astype(_BF16), vwn_ref[0], preferred_element_type=_F32)) / jnp.maximum(l_w, 1.0)

    for cp in copies:
        cp.wait()
    k_sel = ksel_scr[...].reshape(_NSA_KV, past).astype(_BF16)
    v_sel = vsel_scr[...].reshape(_NSA_KV, past).astype(_BF16)
    s_s = (jnp.dot(q, k_sel, preferred_element_type=_F32)
           + jnp.dot(bias, e_ref[...], preferred_element_type=_F32))
    s_sn = jnp.where(new_ok, lax.dot_general(q, ksn_ref[0], _NT, preferred_element_type=_F32), _NEG)
    m_s = jnp.maximum(jnp.max(s_s, -1, keepdims=True), jnp.max(s_sn, -1, keepdims=True))
    p_s = jnp.exp(s_s - m_s)
    p_sn = jnp.where(new_ok, jnp.exp(s_sn - m_s), 0.0)
    l_s = jnp.sum(p_s, -1, keepdims=True) + jnp.sum(p_sn, -1, keepdims=True)
    o_s = (lax.dot_general(p_s.astype(_BF16), v_sel, _NT, preferred_element_type=_F32)
           + jnp.dot(p_sn.astype(_BF16), vsn_ref[0], preferred_element_type=_F32)) / jnp.maximum(l_s, 1.0)

    g = g_ref[0]
    o_ref[0] = g[:, 0:1] * o_c + g[:, 1:2] * o_s + g[:, 2:3] * o_w


def _nsa_sample(y, kcb, vcb, pool_k, pool_v, win_k, win_v, page_table, b, ts):
    n_pages = page_table.shape[1]
    past = n_pages * _PAGE
    n_past = past // _SEL_BLOCK
    assert ts <= _SEL_BLOCK and _SEL_TOPN - 1 <= n_past <= 128
    rows = _NSA_KVH * _NSA_G * ts
    col = lambda c, w: y[:, c:c + w]
    q = _block_diag_queries(col(_C_NQ, _NSA_Q).reshape(b, ts, _NSA_H, _NSA_DH))
    g = _gate_rows(col(_C_NG, 3 * _NSA_H).reshape(b, ts, _NSA_KVH, _NSA_G, 3))
    new = lambda c: col(c, _NSA_KV).astype(_BF16).reshape(b, ts, _NSA_KV)
    expand = (jnp.arange(past)[None, :] // _SEL_BLOCK == jnp.arange(n_past)[:, None]).astype(_BF16)
    per_b = lambda a: pl.BlockSpec((1,) + a.shape[1:], lambda i, pt, nd=a.ndim: (i,) + (0,) * (nd - 1))
    win_t = lambda a: _pool_token_minor(a).reshape(b, _NSA_KV, a.shape[1])
    blocked = [q, g, kcb.astype(_BF16), vcb.astype(_BF16), new(_C_KS), new(_C_VS),
               win_t(win_k), win_t(win_v), new(_C_KW), new(_C_VW)]
    pools = [_pool_token_minor(pool_k), _pool_token_minor(pool_v)]
    grid_spec = pltpu.PrefetchScalarGridSpec(
        num_scalar_prefetch=1,
        grid=(b,),
        in_specs=[per_b(a) for a in blocked] + [pl.BlockSpec(expand.shape, lambda i, pt: (0, 0))]
        + [pl.BlockSpec(memory_space=pl.ANY)] * 2,
        out_specs=pl.BlockSpec((1, rows, _NSA_KV), lambda i, pt: (i, 0, 0)),
        scratch_shapes=[pltpu.VMEM((_NSA_KVH, _NSA_DH, past), _F32), pltpu.VMEM((_NSA_KVH, _NSA_DH, past), _F32),
                        pltpu.SemaphoreType.DMA((2,))],
    )
    o = pl.pallas_call(
        functools.partial(_nsa_sample_body, past=past, ts=ts),
        grid_spec=grid_spec,
        out_shape=jax.ShapeDtypeStruct((b, rows, _NSA_KV), _F32),
        compiler_params=_cparams(("arbitrary",)),
        name="nsa_sample",
    )(page_table, *blocked, expand, *pools)
    return _own_head_rows(o, ts).reshape(b * ts, _NSA_Q)


def _merge_body(o_ref, srg_ref, on_ref, mg0_ref, mg1_ref, x_ref, g1_ref, wbr_ref, wbn_ref, wout_ref,
                lng_ref, lnb_ref, out_ref):
    o = o_ref[...]
    parts = []
    for h in range(_RET_H):
        seg = o[:, h * _RET_DV:(h + 1) * _RET_DV]
        mu = jnp.mean(seg, -1, keepdims=True)
        d = seg - mu
        var = jnp.mean(d * d, -1, keepdims=True)
        parts.append(d * lax.rsqrt(var + _LN_EPS))
    a = (jnp.concatenate(parts, axis=1) * srg_ref[...]).astype(_BF16)
    br = jnp.dot(a, wbr_ref[...], preferred_element_type=_F32)
    bn = jnp.dot(on_ref[...].astype(_BF16), wbn_ref[...], preferred_element_type=_F32)
    m = (mg0_ref[...] * br + mg1_ref[...] * bn).astype(_BF16)
    yv = jnp.dot(m, wout_ref[...], preferred_element_type=_F32)
    r = _ALPHA * x_ref[...] + g1_ref[0] * yv
    out_ref[...] = _layer_norm(r, lng_ref[...], lnb_ref[...])


def _merge(o_r, y, o_n, x2, g13, mod_map, w_br_ret, w_br_nsa, w_out, ln_g, ln_b, tm):
    n = x2.shape[0]
    r = g13.shape[1]
    row = lambda c: pl.BlockSpec((tm, _D), lambda i: (i, c))
    wbr, wbn, wout = w_br_ret.astype(_BF16), w_br_nsa.astype(_BF16), w_out.astype(_BF16)
    lng, lnb = ln_g.reshape(1, _D), ln_b.reshape(1, _D)
    return pl.pallas_call(
        _merge_body,
        grid=(n // tm,),
        in_specs=[row(0), row(_C_RG // _D), pl.BlockSpec((tm, _NSA_Q), lambda i: (i, 0)),
                  row(_C_MG // _D), row(_C_MG // _D + 1), row(0),
                  pl.BlockSpec((1, r, _D), lambda i: (mod_map(i), 0, 0))]
        + [_full_spec(a, 1) for a in (wbr, wbn, wout, lng, lnb)],
        out_specs=row(0),
        out_shape=jax.ShapeDtypeStruct((n, _D), _F32),
        compiler_params=_cparams(("parallel",)),
        name="merge_ln1",
    )(o_r, y, o_n, y, y, x2, g13, wbr, wbn, wout, lng, lnb)


def _topk_rows(work, k):
    r = work.shape[0]
    iota = lax.broadcasted_iota(jnp.int32, work.shape, 0)
    vals = []
    for _ in range(k):
        mx = jnp.max(work, axis=0, keepdims=True)
        idx = jnp.min(jnp.where(work == mx, iota, r), axis=0, keepdims=True)
        work = jnp.where(iota == idx, -jnp.inf, work)
        vals.append(mx)
    return vals


def _peer_select_body(x_ref, sc_ref, sh_ref, wq_ref, sk_ref, xb_ref, st_ref, thr_ref, mz_ref, sv_scr):
    xb = (x_ref[...] * (1.0 + sc_ref[0]) + sh_ref[0]).astype(_BF16)
    xb_ref[...] = xb
    q = jnp.dot(xb, wq_ref[...], preferred_element_type=_F32).astype(_BF16)
    k = _PEER_TOPK
    neg = jnp.full((1, q.shape[0]), -jnp.inf, _F32)
    for h in range(_PEER_H):
        for p in range(2):
            hp = 2 * h + p
            st = lax.dot_general(sk_ref[hp], q[:, hp * _PEER_HALF:(hp + 1) * _PEER_HALF], _NT,
                                 preferred_element_type=_F32)
            st_ref[hp] = st
            for r, v in enumerate(_topk_rows(st, k)):
                sv_scr[p, r:r + 1, :] = v
        sv0, sv1 = sv_scr[0], sv_scr[1]
        row8 = lax.broadcasted_iota(jnp.int32, (8, 1), 0)
        row16 = lax.broadcasted_iota(jnp.int32, (k, 1), 0)
        groups = [sv0[0:1] + sv1, jnp.where(row16 == 0, neg, sv0 + sv1[0:1])]
        for a in range(1, 8):
            groups.append(jnp.where(row8 == 0, neg, sv0[a:a + 1] + sv1[0:8]))
        top = _topk_rows(jnp.concatenate(groups, axis=0), k)
        m = top[0]
        z = functools.reduce(lambda u, v: u + v, [jnp.exp(t - m) for t in top])
        thr_ref[h:h + 1, :] = top[k - 1]
        mz_ref[h:h + 1, :] = m + jnp.log(z)


def _peer_select(x1, sc3, sh3, mod_map, wq, subkeys, tn):
    n = x1.shape[0]
    r = sc3.shape[1]
    hp = 2 * _PEER_H
    wqb = wq.astype(_BF16)
    sk = subkeys.astype(_BF16).reshape(hp, _PEER_NK, _PEER_HALF)
    mod = pl.BlockSpec((1, r, _D), lambda i: (mod_map(i), 0, 0))
    return pl.pallas_call(
        _peer_select_body,
        grid=(n // tn,),
        in_specs=[pl.BlockSpec((tn, _D), lambda i: (i, 0)), mod, mod, _full_spec(wqb, 1), _full_spec(sk, 1)],
        out_specs=[pl.BlockSpec((tn, _D), lambda i: (i, 0)),
                   pl.BlockSpec((hp, _PEER_NK, tn), lambda i: (0, 0, i)),
                   pl.BlockSpec((_PEER_H, tn), lambda i: (0, i)),
                   pl.BlockSpec((_PEER_H, tn), lambda i: (0, i))],
        out_shape=[jax.ShapeDtypeStruct((n, _D), _BF16),
                   jax.ShapeDtypeStruct((hp, _PEER_NK, n), _F32),
                   jax.ShapeDtypeStruct((_PEER_H, n), _F32),
                   jax.ShapeDtypeStruct((_PEER_H, n), _F32)],
        scratch_shapes=[pltpu.VMEM((2, _PEER_TOPK, tn), _F32)],
        compiler_params=_cparams(("parallel",)),
        name="peer_select",
    )(x1, sc3, sh3, wqb, sk)


def _peer_main_body(xb_ref, u_ref, v_ref, st_ref, thr_ref, mz_ref, x_ref, g2_ref, lng_ref, lnb_ref,
                    out_ref, acc_ref, e_scr, g_scr):
    e = pl.program_id(1)
    slabs = u_ref.shape[0] // _PEER_NK
    n = xb_ref.shape[0]

    @pl.when(e == 0)
    def _():
        acc_ref[...] = jnp.zeros_like(acc_ref)
        for h in range(_PEER_H):
            s1 = st_ref[2 * h + 1]
            m1 = jnp.max(s1, axis=0, keepdims=True)
            e_scr[2 * h] = jnp.exp(st_ref[2 * h] - (mz_ref[h:h + 1, :] - m1))
            e_scr[2 * h + 1] = jnp.exp(s1 - m1)

    for sl in range(slabs):
        i = e * slabs + sl
        gt = jnp.zeros((_PEER_NK, n), _F32)
        for h in range(_PEER_H):
            c = st_ref[2 * h, pl.ds(i, 1), :] + st_ref[2 * h + 1]
            gt = jnp.where(c >= thr_ref[h:h + 1, :], gt + e_scr[2 * h, pl.ds(i, 1), :] * e_scr[2 * h + 1], gt)
        g_scr[sl * _PEER_NK:(sl + 1) * _PEER_NK, :] = gt
    act = _gelu(lax.dot_general(u_ref[...], xb_ref[...], _NT, preferred_element_type=_F32))
    wt = (act * g_scr[...]).astype(_BF16)
    acc_ref[...] += lax.dot_general(wt, v_ref[...], _TN_DIMS, preferred_element_type=_F32)

    @pl.when(e == pl.num_programs(1) - 1)
    def _():
        r = _ALPHA * x_ref[...] + g2_ref[0] * acc_ref[...]
        out_ref[...] = _layer_norm(r, lng_ref[...], lnb_ref[...])


def _peer_main(xb, u_b, v_b, st, thr, mz, x1, g23, mod_map, ln_g, ln_b, tn, ec):
    n = x1.shape[0]
    r = g23.shape[1]
    ne = u_b.shape[0]
    hp = 2 * _PEER_H
    lng, lnb = ln_g.reshape(1, _D), ln_b.reshape(1, _D)
    return pl.pallas_call(
        _peer_main_body,
        grid=(n // tn, ne // ec),
        in_specs=[pl.BlockSpec((tn, _D), lambda i, e: (i, 0)),
                  pl.BlockSpec((ec, _D), lambda i, e: (e, 0)),
                  pl.BlockSpec((ec, _D), lambda i, e: (e, 0)),
                  pl.BlockSpec((hp, _PEER_NK, tn), lambda i, e: (0, 0, i)),
                  pl.BlockSpec((_PEER_H, tn), lambda i, e: (0, i)),
                  pl.BlockSpec((_PEER_H, tn), lambda i, e: (0, i)),
                  pl.BlockSpec((tn, _D), lambda i, e: (i, 0)),
                  pl.BlockSpec((1, r, _D), lambda i, e: (mod_map(i), 0, 0)),
                  _full_spec(lng, 2), _full_spec(lnb, 2)],
        out_specs=pl.BlockSpec((tn, _D), lambda i, e: (i, 0)),
        out_shape=jax.ShapeDtypeStruct((n, _D), _F32),
        scratch_shapes=[pltpu.VMEM((tn, _D), _F32), pltpu.VMEM((hp, _PEER_NK, tn), _F32),
                        pltpu.VMEM((ec, tn), _F32)],
        compiler_params=_cparams(("parallel", "arbitrary")),
        name="peer_main",
    )(xb, u_b, v_b, st, thr, mz, x1, g23, lng, lnb)


_PEER_SEL_TOKENS = 256
_PEER_MAIN_TOKENS = 512
_PEER_EXPERT_CHUNK = 1024


def _peer_and_ln(x1, sc3, sh3, sel_map, g3, main_map, wq, subkeys, u_b, v_b, ln_g, ln_b):
    n = x1.shape[0]
    xb, st, thr, mz = _peer_select(x1, sc3, sh3, sel_map, wq, subkeys, min(_PEER_SEL_TOKENS, n))
    return _peer_main(xb, u_b, v_b, st, thr, mz, x1, g3, main_map, ln_g, ln_b, min(_PEER_MAIN_TOKENS, n),
                      _PEER_EXPERT_CHUNK)


def kernel(x_prompt, x_sample, c_prompt, c_sample, state_ret, cache_cmp_k, cache_cmp_v, cache_sel_k, cache_sel_v,
           cache_win_k, cache_win_v, page_table, w_ada, b_ada, w_in, cmp_pe_k, cmp_w1_k, cmp_w2_k, cmp_pe_v,
           cmp_w1_v, cmp_w2_v, w_br_ret, w_br_nsa, w_out, ln1_g, ln1_b, peer_wq, peer_subkeys, peer_u, peer_v,
           ln2_g, ln2_b):
    l = 0
    B, T, _ = x_prompt.shape
    Bs, Ts, _ = x_sample.shape
    n_pages = page_table.shape[1]
    past = n_pages * _PAGE
    wcat = _build_wcat(w_in[l])
    u_b, v_b = peer_u[l].astype(_BF16), peer_v[l].astype(_BF16)
    cw_k = _compress_weights(cmp_pe_k[l], cmp_w1_k[l], cmp_w2_k[l])
    cw_v = _compress_weights(cmp_pe_v[l], cmp_w1_v[l], cmp_w2_v[l])
    stride_w = _CMP_STRIDE * _NSA_KV
    cache5 = lambda a, b, t: a.reshape(b, t, _NSA_KVH, _NSA_DH)

    ns = Bs * Ts
    tms = min(512, ns)
    ada_s = _linear(c_sample, w_ada[l], b_ada[l])
    rowmod = lambda a, tile: jnp.broadcast_to(a[:, None, :], (Bs, Ts, _D)).reshape(ns // tile, tile, _D)
    mods = jnp.split(ada_s, 6, axis=-1)
    xs2 = x_sample.reshape(ns, _D)
    pos = past + jnp.arange(Ts, dtype=jnp.int32)
    tabs_s = _rope_tables(jnp.tile(pos, tms // Ts))
    ident = lambda i: i
    ys = _inproj(xs2, rowmod(mods[1], tms), rowmod(mods[0], tms), ident, wcat, tabs_s, lambda i: 0, tms)
    cols = lambda c, w: ys[:, c:c + w]
    assert Ts < _CMP_STRIDE
    kcb_s = _compress_paged(_pool_token_minor(cache_cmp_k[l]), page_table, cw_k).reshape(Bs, -1, _NSA_KV)
    vcb_s = _compress_paged(_pool_token_minor(cache_cmp_v[l]), page_table, cw_v).reshape(Bs, -1, _NSA_KV)
    o_ns = _nsa_sample(ys, kcb_s, vcb_s, cache_sel_k[l], cache_sel_v[l], cache_win_k[l], cache_win_v[l],
                       page_table, Bs, Ts)
    o_rs, r_new = _retention_sample(ys, state_ret[l], Bs, Ts)
    tms2 = min(256, ns)
    x1s = _merge(o_rs, ys, o_ns, xs2, rowmod(mods[2], tms2), ident, w_br_ret[l], w_br_nsa[l], w_out[l],
                 ln1_g[l], ln1_b[l], tms2)
    sel_t, main_t = min(_PEER_SEL_TOKENS, ns), min(_PEER_MAIN_TOKENS, ns)
    xs_out = _peer_and_ln(x1s, rowmod(mods[4], sel_t), rowmod(mods[3], sel_t), ident, rowmod(mods[5], main_t), ident,
                          peer_wq[l], peer_subkeys[l], u_b, v_b, ln2_g[l], ln2_b[l])
    new5 = lambda c: cache5(cols(c, _NSA_KV), Bs, Ts)
    kwin = jnp.concatenate([cache_win_k[l], new5(_C_KW)], axis=1)
    vwin = jnp.concatenate([cache_win_v[l], new5(_C_VW)], axis=1)
    sample_state = (r_new, new5(_C_KC), new5(_C_VC), new5(_C_KS), new5(_C_VS), kwin[:, Ts:], vwin[:, Ts:])

    tm = 512
    tpb = T // tm
    ada_p = _linear(c_prompt, w_ada[l], b_ada[l])
    sh1, sc1, g1, sh2, sc2, g2 = [a.reshape(B, 1, _D) for a in jnp.split(ada_p, 6, axis=-1)]
    xp2 = x_prompt.reshape(B * T, _D)
    tabs_p = _rope_tables(jnp.arange(T, dtype=jnp.int32))
    y = _inproj(xp2, sc1, sh1, lambda i: i // tpb, wcat, tabs_p, lambda i: i % tpb, tm)
    col = lambda c, w: y[:, c:c + w]
    o_r, R_p = _retention_prompt(y, B, T)
    nc = T // _CMP_STRIDE
    kcb = _compress_prompt(col(_C_KC, _NSA_KV).reshape(B * nc, stride_w), B, cw_k).reshape(B, nc, _NSA_KV)
    vcb = _compress_prompt(col(_C_VC, _NSA_KV).reshape(B * nc, stride_w), B, cw_v).reshape(B, nc, _NSA_KV)
    o_n = _nsa_prompt(y, kcb, vcb, B, T)
    tm2 = 256
    tpb2 = T // tm2
    bmap2 = lambda i: i // tpb2
    x1 = _merge(o_r, y, o_n, xp2, g1, bmap2, w_br_ret[l], w_br_nsa[l], w_out[l], ln1_g[l], ln1_b[l], tm2)
    xp_out = _peer_and_ln(x1, sc2, sh2, lambda i: i // (T // _PEER_SEL_TOKENS), g2,
                          lambda i: i // (T // _PEER_MAIN_TOKENS), peer_wq[l], peer_subkeys[l], u_b, v_b,
                          ln2_g[l], ln2_b[l])
    wb = min(_WINDOW, T)
    seq5 = lambda c: cache5(col(c, _NSA_KV), B, T)
    prompt_state = (R_p, seq5(_C_KC), seq5(_C_VC), seq5(_C_KS), seq5(_C_VS),
                    seq5(_C_KW)[:, T - wb:], seq5(_C_VW)[:, T - wb:])

    outs =[xp_out.reshape(B, T, _D), xs_out.reshape(Bs, Ts, _D)]
    outs += [a[None] for a in prompt_state]
    outs += [a[None] for a in sample_state]
    return tuple(outs)
```

```python
import functools
import math

import jax
import jax.numpy as jnp
import numpy as np
from jax import lax
from jax.experimental import pallas as pl
from jax.experimental.pallas import tpu as pltpu

_F32 = jnp.float32
_BF16 = jnp.bfloat16

_D = 1024
_RET_H, _RET_DK, _RET_DV, _RET_C = 4, 128, 256, 128
_NSA_H, _NSA_KVH, _NSA_G, _NSA_DH = 8, 2, 4, 64
_CMP_BLOCK, _CMP_STRIDE, _CMP_HID = 32, 16, 128
_SEL_BLOCK, _SEL_TOPN, _WINDOW, _QB = 64, 16, 512, 64
_FORCED = 1.0e4
_PEER_H, _PEER_NK, _PEER_HALF, _PEER_TOPK = 8, 128, 128, 16
_PAGE = 128
_ALPHA = 2.0 ** 0.25
_LN_EPS = 1e-5
_NEG = -1e30
_ROPE_THETA = 10000.0
_RET_QK = _RET_H * _RET_DK
_RET_V = _RET_H * _RET_DV
_NSA_Q = _NSA_H * _NSA_DH
_NSA_KV = _NSA_KVH * _NSA_DH
_IN_SECTIONS = (_RET_QK, _RET_QK, _RET_V, _RET_V, _NSA_Q, 6 * _NSA_KV, 3 * _NSA_H, 2 * _D)
_IN_SPLITS = tuple(int(s) for s in np.cumsum(_IN_SECTIONS)[:-1])
_RET_LOG_G = tuple(math.log1p(-2.0 ** (-5.0 - h)) for h in range(_RET_H))

_VMEM_LIMIT = 48 * 1024 * 1024
_INPROJ_VMEM_LIMIT = 56 * 1024 * 1024

_TN = 512
_C_RQ, _C_RK, _C_RV, _C_RG, _C_MG, _C_NQ = 0, 512, 1024, 2048, 3072, 5120
_C_KC, _C_KS, _C_KW = 5632, 5760, 5888
_C_VC, _C_VS, _C_VW, _C_NG = 6144, 6272, 6400, 6528
_IN_COLS = 13 * _TN

_SEL_CHUNK = 512
_NSA_ROWS = _NSA_KVH * _NSA_G * _QB

_NT = (((1,), (1,)), ((), ()))
_TN_DIMS = (((0,), (0,)), ((), ()))


def _cparams(sem):
    return pltpu.CompilerParams(dimension_semantics=sem, vmem_limit_bytes=_VMEM_LIMIT)


def _gelu(x):
    return 0.5 * x * (1.0 + lax.erf(x * (1.0 / math.sqrt(2.0))))


def _layer_norm(r, g, b):
    mu = jnp.mean(r, -1, keepdims=True)
    d = r - mu
    var = jnp.mean(d * d, -1, keepdims=True)
    return d * lax.rsqrt(var + _LN_EPS) * g + b


def _masked_softmax(s, mask):
    s = jnp.where(mask, s, _NEG)
    p = jnp.where(mask, jnp.exp(s - jnp.max(s, -1, keepdims=True)), 0.0)
    return p / jnp.maximum(jnp.sum(p, -1, keepdims=True), 1.0)


def _full_spec(a, grid_rank):
    zeros = (0,) * a.ndim
    return pl.BlockSpec(a.shape, {1: lambda i: zeros, 2: lambda i, j: zeros}[grid_rank])


def _linear_body(a_ref, w_ref, b_ref, o_ref):
    o_ref[...] = jnp.dot(a_ref[...].astype(_BF16), w_ref[...], preferred_element_type=_F32) + b_ref[...]


def _linear(a, w, b, tn=512):
    m, k = a.shape
    n = w.shape[1]
    return pl.pallas_call(
        _linear_body,
        grid=(n // tn,),
        in_specs=[pl.BlockSpec((m, k), lambda j: (0, 0)),
                  pl.BlockSpec((k, tn), lambda j: (0, j)),
                  pl.BlockSpec((1, tn), lambda j: (0, j))],
        out_specs=pl.BlockSpec((m, tn), lambda j: (0, j)),
        out_shape=jax.ShapeDtypeStruct((m, n), _F32),
        compiler_params=_cparams(("arbitrary",)),
        name="ada_linear",
    )(a, w.astype(_BF16), b.reshape(1, n))


def _build_wcat(w_in):
    rq, rk, rv, rg, nq, nkv, ng, mg = jnp.split(w_in, _IN_SPLITS, axis=1)
    nkv6 = nkv.reshape(_D, 6, _NSA_KV)
    kc, vc, ks, vs, kw, vw = [nkv6[:, i] for i in range(6)]
    z = jnp.zeros((_D, _NSA_KV), w_in.dtype)
    ngp = jnp.pad(ng, ((0, 0), (0, _NSA_KV - ng.shape[1])))
    wcat = jnp.concatenate([rq, rk, rv, rg, mg, nq, kc, ks, kw, z, vc, vs, vw, ngp], axis=1)
    return wcat.astype(_BF16)


def _rope_tables(pos):
    out = []
    for half in (_RET_DK // 2, _NSA_DH // 2):
        inv = _ROPE_THETA ** (-jnp.arange(half, dtype=_F32) / half)
        ang = pos.astype(_F32)[:, None] * inv[None, :]
        cos, sin = jnp.cos(ang), jnp.sin(ang)
        reps = _RET_DK // (2 * half)
        out.append(jnp.tile(jnp.concatenate([cos, cos], -1), (1, reps)))
        out.append(jnp.tile(jnp.concatenate([-sin, sin], -1), (1, reps)))
    return out


def _inproj_body(x_ref, sc_ref, sh_ref, w_ref, c128_ref, s128_ref, c64_ref, s64_ref, y_ref):
    a = (x_ref[...] * (1.0 + sc_ref[0]) + sh_ref[0]).astype(_BF16)
    lane = lax.broadcasted_iota(jnp.int32, (a.shape[0], _TN), 1)
    reps = _TN // _RET_DK

    def rope(y, half, c_ref, s_ref):
        first = (lane % (2 * half)) < half
        partner = jnp.where(first, pltpu.roll(y, _TN - half, 1), pltpu.roll(y, half, 1))
        return y * jnp.tile(c_ref[...], (1, reps)) + partner * jnp.tile(s_ref[...], (1, reps))

    for j in range(_IN_COLS // _TN):
        c0 = j * _TN
        y = jnp.dot(a, w_ref[:, c0:c0 + _TN], preferred_element_type=_F32)
        if c0 == _C_RQ:
            y = rope(y, _RET_DK // 2, c128_ref, s128_ref)
        elif c0 == _C_RK:
            y = rope(y, _RET_DK // 2, c128_ref, s128_ref) * (_RET_DK ** -0.5)
        elif _C_RG <= c0 < _C_MG:
            y = y * jax.nn.sigmoid(y)
        elif _C_MG <= c0 < _C_NQ:
            y = jax.nn.sigmoid(y)
        elif _C_NQ <= c0 < _C_VC:
            y = rope(y, _NSA_DH // 2, c64_ref, s64_ref)
        elif c0 == _C_VC:
            y = jnp.where(lane < _C_NG - _C_VC, y, jax.nn.sigmoid(y))
        y_ref[:, c0:c0 + _TN] = y


def _inproj(x2, sc3, sh3, mod_map, wcat, tabs, tab_map, tm):
    n = x2.shape[0]
    r = sc3.shape[1]
    tab_spec = pl.BlockSpec((tm, _RET_DK), lambda i: (tab_map(i), 0))
    return pl.pallas_call(
        _inproj_body,
        grid=(n // tm,),
        in_specs=[pl.BlockSpec((tm, _D), lambda i: (i, 0)),
                  pl.BlockSpec((1, r, _D), lambda i: (mod_map(i), 0, 0)),
                  pl.BlockSpec((1, r, _D), lambda i: (mod_map(i), 0, 0)),
                  pl.BlockSpec((_D, _IN_COLS), lambda i: (0, 0), pipeline_mode=pl.Buffered(1)),
                  tab_spec, tab_spec, tab_spec, tab_spec],
        out_specs=pl.BlockSpec((tm, _IN_COLS), lambda i: (i, 0)),
        out_shape=jax.ShapeDtypeStruct((n, _IN_COLS), _F32),
        compiler_params=pltpu.CompilerParams(dimension_semantics=("parallel",),
                                             vmem_limit_bytes=_INPROJ_VMEM_LIMIT),
        name="in_proj",
    )(x2, sc3, sh3, wcat, *tabs)


def _ret_consts(c):
    lg = jnp.asarray(_RET_LOG_G, _F32)[:, None, None]
    n = jnp.arange(c, dtype=_F32)
    diff = n[:, None] - n[None, :]
    decay = jnp.where(diff >= 0, jnp.exp(lg * jnp.maximum(diff, 0.0)), 0.0)
    cs = jnp.broadcast_to(jnp.exp(lg * (n[:, None] + 1.0)), (_RET_H, c, _RET_DV))
    kd = jnp.broadcast_to(jnp.exp(lg * (c - 1.0 - n[:, None])), (_RET_H, c, _RET_DK))
    return decay, cs, kd


def _ret_step(q_ref, k_ref, v_ref, dec_ref, cs_ref, kd_ref, o_ref, state, chunk):
    new = []
    for h in range(_RET_H):
        q = q_ref[:, h * _RET_DK:(h + 1) * _RET_DK].astype(_BF16)
        k = k_ref[:, h * _RET_DK:(h + 1) * _RET_DK]
        v = v_ref[:, h * _RET_DV:(h + 1) * _RET_DV].astype(_BF16)
        r = state(h)
        s = lax.dot_general(q, k.astype(_BF16), _NT, preferred_element_type=_F32) * dec_ref[h]
        inner = jnp.dot(s.astype(_BF16), v, preferred_element_type=_F32)
        cross = jnp.dot(q, r.astype(_BF16), preferred_element_type=_F32) * cs_ref[h]
        o_ref[:, h * _RET_DV:(h + 1) * _RET_DV] = inner + cross
        k_dec = (k * kd_ref[h]).astype(_BF16)
        new.append(r * math.exp(_RET_LOG_G[h] * chunk)
                   + lax.dot_general(k_dec, v, _TN_DIMS, preferred_element_type=_F32))
    return new


def _ret_prompt_body(q_ref, k_ref, v_ref, dec_ref, cs_ref, kd_ref, o_ref, r_ref):
    @pl.when(pl.program_id(1) == 0)
    def _():
        r_ref[...] = jnp.zeros_like(r_ref)

    new = _ret_step(q_ref, k_ref, v_ref, dec_ref, cs_ref, kd_ref, o_ref, lambda h: r_ref[0, h], _RET_C)
    for h in range(_RET_H):
        r_ref[0, h] = new[h]


def _retention_prompt(y, b, t):
    c = _RET_C
    nc = t // c
    consts = _ret_consts(c)
    return pl.pallas_call(
        _ret_prompt_body,
        grid=(b, nc),
        in_specs=[pl.BlockSpec((c, _RET_QK), lambda i, j: (i * nc + j, _C_RQ // _RET_QK)),
                  pl.BlockSpec((c, _RET_QK), lambda i, j: (i * nc + j, _C_RK // _RET_QK)),
                  pl.BlockSpec((c, _RET_V), lambda i, j: (i * nc + j, _C_RV // _RET_V))]
        + [_full_spec(a, 2) for a in consts],
        out_specs=[pl.BlockSpec((c, _RET_V), lambda i, j: (i * nc + j, 0)),
                   pl.BlockSpec((1, _RET_H, _RET_DK, _RET_DV), lambda i, j: (i, 0, 0, 0))],
        out_shape=[jax.ShapeDtypeStruct((b * t, _RET_V), _F32),
                   jax.ShapeDtypeStruct((b, _RET_H, _RET_DK, _RET_DV), _F32)],
        compiler_params=_cparams(("parallel", "arbitrary")),
        name="retention_prompt",
    )(y, y, y, *consts)


def _ret_sample_body(q_ref, k_ref, v_ref, st_ref, dec_ref, cs_ref, kd_ref, o_ref, r_ref, *, chunk):
    new = _ret_step(q_ref, k_ref, v_ref, dec_ref, cs_ref, kd_ref, o_ref, lambda h: st_ref[0, h], chunk)
    for h in range(_RET_H):
        r_ref[0, h] = new[h]


def _retention_sample(y, state, b, t):
    consts = _ret_consts(t)
    st_spec = pl.BlockSpec((1, _RET_H, _RET_DK, _RET_DV), lambda i: (i, 0, 0, 0))
    return pl.pallas_call(
        functools.partial(_ret_sample_body, chunk=t),
        grid=(b,),
        in_specs=[pl.BlockSpec((t, _RET_QK), lambda i: (i, _C_RQ // _RET_QK)),
                  pl.BlockSpec((t, _RET_QK), lambda i: (i, _C_RK // _RET_QK)),
                  pl.BlockSpec((t, _RET_V), lambda i: (i, _C_RV // _RET_V)),
                  st_spec] + [_full_spec(a, 1) for a in consts],
        out_specs=[pl.BlockSpec((t, _RET_V), lambda i: (i, 0)), st_spec],
        out_shape=[jax.ShapeDtypeStruct((b * t, _RET_V), _F32),
                   jax.ShapeDtypeStruct(state.shape, _F32)],
        compiler_params=_cparams(("parallel",)),
        name="retention_sample",
    )(y, y, y, state, *consts)


def _compress_weights(pe, w1, w2):
    w = w1.reshape(2, _CMP_STRIDE, _NSA_DH, _CMP_HID)
    eye = jnp.eye(_NSA_KVH, dtype=w1.dtype)
    wf = jnp.einsum('psde,kl->skdlpe', w, eye).reshape(_CMP_STRIDE * _NSA_KV, _NSA_KVH * 2 * _CMP_HID)
    pe8 = jnp.pad(pe.reshape(1, -1), ((0, 7), (0, 0)))
    return wf.astype(_BF16), pe8.astype(_BF16), w1.astype(_BF16), w2.astype(_BF16)


def _compress_compute(rows, wf_ref, pe_ref, w1_ref, w2_ref, o_ref, z_scr):
    ns = rows.shape[0]
    z = jnp.dot(rows.astype(_BF16), wf_ref[...], preferred_element_type=_F32)
    z_scr[pl.ds(0, ns), :] = z
    z_scr[pl.ds(ns, 8), :] = jnp.zeros((8, z.shape[1]), _F32)
    bias = jnp.dot(pe_ref[...], w1_ref[...], preferred_element_type=_F32)[0:1]
    for k in range(_NSA_KVH):
        c0 = k * 2 * _CMP_HID
        z_next = z_scr[pl.ds(1, ns), c0 + _CMP_HID:c0 + 2 * _CMP_HID]
        hid = _gelu(z[:, c0:c0 + _CMP_HID] + z_next + bias)
        o_ref[:, k * _NSA_DH:(k + 1) * _NSA_DH] = jnp.dot(hid.astype(_BF16), w2_ref[...],
                                                         preferred_element_type=_F32)


def _compress_body(rows_ref, wf_ref, pe_ref, w1_ref, w2_ref, o_ref, z_scr):
    _compress_compute(rows_ref[...], wf_ref, pe_ref, w1_ref, w2_ref, o_ref, z_scr)


def _compress_prompt(rows2, b, weights):
    ns = rows2.shape[0] // b
    return pl.pallas_call(
        _compress_body,
        grid=(b,),
        in_specs=[pl.BlockSpec((ns, rows2.shape[1]), lambda i: (i, 0))] + [_full_spec(a, 1) for a in weights],
        out_specs=pl.BlockSpec((ns, _NSA_KV), lambda i: (i, 0)),
        out_shape=jax.ShapeDtypeStruct((b * ns, _NSA_KV), _F32),
        scratch_shapes=[pltpu.VMEM((ns + 8, _NSA_KVH * 2 * _CMP_HID), _F32)],
        compiler_params=_cparams(("parallel",)),
        name="compress_prompt",
    )(rows2, *weights)


def _pool_token_minor(pool):
    return jnp.transpose(pool, (0, 2, 3, 1))


def _page_copies(pt_ref, b, pool_ref, dst_ref, sem):
    n_pages = pt_ref.shape[1]
    return [pltpu.make_async_copy(pool_ref.at[pt_ref[b, p]], dst_ref.at[:, :, pl.ds(p * _PAGE, _PAGE)], sem)
            for p in range(n_pages)]


def _prefetched_pages(copies_of):
    b = pl.program_id(0)
    slot = b % 2

    @pl.when(b == 0)
    def _():
        for cp in copies_of(b, slot):
            cp.start()

    @pl.when(b + 1 < pl.num_programs(0))
    def _():
        for cp in copies_of(b + 1, 1 - slot):
            cp.start()

    for cp in copies_of(b, slot):
        cp.wait()
    return slot


def _compress_paged_body(pt_ref, pool_ref, wf_ref, pe_ref, w1_ref, w2_ref, o_ref, xt_scr, x_scr, rows_scr, z_scr,
                         sem):
    slot = _prefetched_pages(lambda b, s: _page_copies(pt_ref, b, pool_ref, xt_scr.at[s], sem.at[s]))
    past = xt_scr.shape[3]
    ns = past // _CMP_STRIDE
    x_scr[...] = xt_scr[slot].reshape(_NSA_KV, past).T
    for s in range(_CMP_STRIDE):
        rows_scr[:, s * _NSA_KV:(s + 1) * _NSA_KV] = x_scr[pl.ds(s, ns, stride=_CMP_STRIDE), :]
    _compress_compute(rows_scr[...], wf_ref, pe_ref, w1_ref, w2_ref, o_ref, z_scr)


def _compress_paged(pool_t, page_table, weights):
    b, n_pages = page_table.shape
    past = n_pages * _PAGE
    ns = past // _CMP_STRIDE
    grid_spec = pltpu.PrefetchScalarGridSpec(
        num_scalar_prefetch=1,
        grid=(b,),
        in_specs=[pl.BlockSpec(memory_space=pl.ANY)]
        + [pl.BlockSpec(a.shape, lambda i, pt, nd=a.ndim: (0,) * nd) for a in weights],
        out_specs=pl.BlockSpec((ns, _NSA_KV), lambda i, pt: (i, 0)),
        scratch_shapes=[pltpu.VMEM((2, _NSA_KVH, _NSA_DH, past), _F32),
                        pltpu.VMEM((past, _NSA_KV), _F32),
                        pltpu.VMEM((ns, _CMP_STRIDE * _NSA_KV), _F32),
                        pltpu.VMEM((ns + 8, _NSA_KVH * 2 * _CMP_HID), _F32),
                        pltpu.SemaphoreType.DMA((2,))],
    )
    return pl.pallas_call(
        _compress_paged_body,
        grid_spec=grid_spec,
        out_shape=jax.ShapeDtypeStruct((b * ns, _NSA_KV), _F32),
        compiler_params=_cparams(("arbitrary",)),
        name="compress_paged",
    )(page_table, pool_t, *weights)


def _overlap_matrix(n_cmp, n_sel_pad):
    nn = lax.broadcasted_iota(jnp.int32, (n_cmp, n_sel_pad), 0)
    jj = lax.broadcasted_iota(jnp.int32, (n_cmp, n_sel_pad), 1)
    per = _SEL_BLOCK // _CMP_STRIDE
    return ((nn < (jj + 1) * per) & (nn * _CMP_STRIDE + _CMP_BLOCK > jj * _SEL_BLOCK)).astype(_BF16)


def _importance(pcs, ov):
    hi = pcs.astype(_BF16)
    lo = (pcs - hi.astype(_F32)).astype(_BF16)
    return jnp.dot(hi, ov, preferred_element_type=_F32) + jnp.dot(lo, ov, preferred_element_type=_F32)


def _topk_mask_rows(work, k):
    r = work.shape[0]
    iota = lax.broadcasted_iota(jnp.int32, work.shape, 0)
    sel = jnp.zeros(work.shape, _F32)
    for _ in range(k):
        mx = jnp.max(work, axis=0, keepdims=True)
        idx = jnp.min(jnp.where(work == mx, iota, r), axis=0, keepdims=True)
        pick = iota == idx
        sel = jnp.where(pick, 1.0, sel)
        work = jnp.where(pick, -jnp.inf, work)
    return sel


def _nsa_prompt_body(q_ref, g_ref, kc_ref, vc_ref, ks_ref, vs_ref, kw_ref, vw_ref, o_ref, *, seq):
    qb = pl.program_id(1)
    rows = _NSA_ROWS
    n_sel_pad = ks_ref.shape[3] - _NSA_KV
    q = q_ref[0, 0]
    row = lax.broadcasted_iota(jnp.int32, (rows, 1), 0)
    tpos = qb * _QB + row % _QB

    n_cmp = kc_ref.shape[1]
    s_c = lax.dot_general(q, kc_ref[0], _NT, preferred_element_type=_F32)
    n_i = lax.broadcasted_iota(jnp.int32, (1, n_cmp), 1)
    p_c = _masked_softmax(s_c, (n_i * _CMP_STRIDE + _CMP_BLOCK - 1) <= tpos)
    o_c = jnp.dot(p_c.astype(_BF16), vc_ref[0], preferred_element_type=_F32)

    gq = _NSA_G * _QB
    pcs = jnp.concatenate(
        [functools.reduce(lambda u, v: u + v, [p_c[k * gq + g * _QB:k * gq + (g + 1) * _QB] for g in range(_NSA_G)])
         for k in range(_NSA_KVH)], axis=0)
    imp = _importance(pcs, _overlap_matrix(n_cmp, n_sel_pad))
    j_i = lax.broadcasted_iota(jnp.int32, imp.shape, 1)
    forced = (j_i == 0) | (j_i == qb) | (j_i == qb - 1)
    score = jnp.where(j_i <= qb, jnp.where(forced, _FORCED, imp), -1.0)
    score_t = score.T
    sel = (_topk_mask_rows(score_t, _SEL_TOPN) * (score_t >= 0.0).astype(_F32)).T
    bias = jnp.where(sel > 0.5, 0.0, _NEG).astype(_BF16)
    bias = jnp.concatenate([bias[k * _QB:(k + 1) * _QB] for k in range(_NSA_KVH) for _ in range(_NSA_G)], axis=0)

    q_sel = jnp.concatenate([q, bias], axis=1)
    k_i = lax.broadcasted_iota(jnp.int32, (1, _SEL_CHUNK), 1)

    def chunk_step(c, carry, causal):
        m, l, acc = carry
        s = lax.dot_general(q_sel, ks_ref[0, c], _NT, preferred_element_type=_F32)
        if causal:
            s = jnp.where((c * _SEL_CHUNK + k_i) <= tpos, s, _NEG)
        m_new = jnp.maximum(m, jnp.max(s, -1, keepdims=True))
        p = jnp.exp(s - m_new)
        alpha = jnp.exp(m - m_new)
        l = alpha * l + jnp.sum(p, -1, keepdims=True)
        acc = alpha * acc + jnp.dot(p.astype(_BF16), vs_ref[0, c], preferred_element_type=_F32)
        return m_new, l, acc

    init = (jnp.full((rows, 1), _NEG, _F32), jnp.zeros((rows, 1), _F32), jnp.zeros((rows, _NSA_KV), _F32))
    last = qb // (_SEL_CHUNK // _SEL_BLOCK)
    carry = lax.fori_loop(0, last, lambda c, cr: chunk_step(c, cr, False), init)
    _, l_s, acc_s = chunk_step(last, carry, True)
    o_s = acc_s / jnp.maximum(l_s, 1.0)

    wlen = _WINDOW + 2 * _QB
    w0 = pl.multiple_of(jnp.clip(qb * _QB - (_WINDOW + _QB), 0, seq - wlen), _QB)
    s_w = lax.dot_general(q, kw_ref[0, pl.ds(w0, wlen), :], _NT, preferred_element_type=_F32)
    kpos = w0 + lax.broadcasted_iota(jnp.int32, (1, wlen), 1)
    p_w = _masked_softmax(s_w, (kpos <= tpos) & (kpos > tpos - _WINDOW))
    o_w = jnp.dot(p_w.astype(_BF16), vw_ref[0, pl.ds(w0, wlen), :], preferred_element_type=_F32)

    g = g_ref[0, 0]
    o_ref[0, 0] = g[:, 0:1] * o_c + g[:, 1:2] * o_s + g[:, 2:3] * o_w


def _block_diag_queries(nq):
    *lead, t, _, _ = nq.shape
    q = (nq * (_NSA_DH ** -0.5)).astype(_BF16).reshape(*lead, t, _NSA_KVH, _NSA_G, _NSA_DH)
    q = jnp.moveaxis(q, -4, -2)
    eye = jnp.eye(_NSA_KVH, dtype=_BF16)
    q = q[..., None, :] * eye[:, None, None, :, None]
    return q.reshape(*lead, _NSA_KVH * _NSA_G * t, _NSA_KV)


def _gate_rows(ng):
    *lead, t, _, _, _ = ng.shape
    return jnp.moveaxis(ng, -4, -2).reshape(*lead, _NSA_KVH * _NSA_G * t, 3)


def _own_head_rows(o, t):
    lead = o.shape[:-2]
    o = o.reshape(*lead, _NSA_KVH, _NSA_G, t, _NSA_KVH, _NSA_DH)
    o = jnp.stack([o[..., k, :, :, k, :] for k in range(_NSA_KVH)], axis=-4)
    return jnp.moveaxis(o, -2, -4).reshape(*lead, t, _NSA_Q)


def _nsa_prompt(y, kcb, vcb, b, t):
    nqb = t // _QB
    n_sel_pad = 128
    assert nqb <= n_sel_pad and t % _SEL_CHUNK == 0
    col = lambda c, w: y[:, c:c + w]
    q = _block_diag_queries(col(_C_NQ, _NSA_Q).reshape(b, nqb, _QB, _NSA_H, _NSA_DH))
    g = _gate_rows(col(_C_NG, 3 * _NSA_H).reshape(b, nqb, _QB, _NSA_KVH, _NSA_G, 3))
    seq_b = lambda c: col(c, _NSA_KV).astype(_BF16).reshape(b, t, _NSA_KV)
    chunked = lambda a: a.reshape(b, t // _SEL_CHUNK, _SEL_CHUNK, a.shape[-1])
    block_id = (jnp.arange(t)[:, None] // _SEL_BLOCK == jnp.arange(n_sel_pad)[None, :]).astype(_BF16)
    ks = chunked(jnp.concatenate([seq_b(_C_KS), jnp.broadcast_to(block_id, (b, t, n_sel_pad))], axis=-1))
    vs, kw, vw = chunked(seq_b(_C_VS)), seq_b(_C_KW), seq_b(_C_VW)
    kc, vc = kcb.astype(_BF16), vcb.astype(_BF16)
    per_q = lambda last: pl.BlockSpec((1, 1, _NSA_ROWS, last), lambda i, j: (i, j, 0, 0))
    per_b = lambda a: pl.BlockSpec((1,) + a.shape[1:], lambda i, j: (i,) + (0,) * (a.ndim - 1))
    o = pl.pallas_call(
        functools.partial(_nsa_prompt_body, seq=t),
        grid=(b, nqb),
        in_specs=[per_q(_NSA_KV), per_q(3), per_b(kc), per_b(vc), per_b(ks), per_b(vs), per_b(kw), per_b(vw)],
        out_specs=per_q(_NSA_KV),
        out_shape=jax.ShapeDtypeStruct((b, nqb, _NSA_ROWS, _NSA_KV), _F32),
        compiler_params=_cparams(("parallel", "arbitrary")),
        name="nsa_prompt",
    )(q, g, kc, vc, ks, vs, kw, vw)
    return _own_head_rows(o, _QB).reshape(b * t, _NSA_Q)


def _nsa_sample_body(pt_ref, q_ref, g_ref, kc_ref, vc_ref, ksn_ref, vsn_ref, kwc_ref, vwc_ref, kwn_ref, vwn_ref,
                     e_ref, poolk_ref, poolv_ref, o_ref, ksel_scr, vsel_scr, sem, *, past, ts):
    slot = _prefetched_pages(lambda b, s: (_page_copies(pt_ref, b, poolk_ref, ksel_scr.at[s], sem.at[0, s])
                                           + _page_copies(pt_ref, b, poolv_ref, vsel_scr.at[s], sem.at[1, s])))

    rows = _NSA_KVH * _NSA_G * ts
    q = q_ref[0]
    row = lax.broadcasted_iota(jnp.int32, (rows, 1), 0)
    t_new = row % ts
    tpos = past + t_new
    new_ok = lax.broadcasted_iota(jnp.int32, (1, ts), 1) <= t_new

    n_cmp = kc_ref.shape[1]
    s_c = lax.dot_general(q, kc_ref[0], _NT, preferred_element_type=_F32)
    n_i = lax.broadcasted_iota(jnp.int32, (1, n_cmp), 1)
    p_c = _masked_softmax(s_c, (n_i * _CMP_STRIDE + _CMP_BLOCK - 1) <= tpos)
    o_c = jnp.dot(p_c.astype(_BF16), vc_ref[0], preferred_element_type=_F32)

    n_past = e_ref.shape[0]
    gt = _NSA_G * ts
    pcs = jnp.concatenate(
        [functools.reduce(lambda u, v: u + v, [p_c[k * gt + g * ts:k * gt + (g + 1) * ts] for g in range(_NSA_G)])
         for k in range(_NSA_KVH)], axis=0)
    imp = _importance(pcs, _overlap_matrix(n_cmp, n_past))
    j_i = lax.broadcasted_iota(jnp.int32, imp.shape, 1)
    cur = past // _SEL_BLOCK
    score = jnp.where((j_i == 0) | (j_i == cur - 1), _FORCED, imp)
    sel = _topk_mask_rows(score.T, _SEL_TOPN - 1).T
    bias = jnp.where(sel > 0.5, 0.0, _NEG).astype(_BF16)
    bias = jnp.concatenate([bias[k * ts:(k + 1) * ts] for k in range(_NSA_KVH) for _ in range(_NSA_G)], axis=0)

    wb = kwc_ref.shape[2]
    s_w = jnp.dot(q, kwc_ref[0].astype(_BF16), preferred_element_type=_F32)
    w_i = lax.broadcasted_iota(jnp.int32, (1, wb), 1)
    w_ok = (past - wb + w_i) > (tpos - _WINDOW)
    s_w = jnp.where(w_ok, s_w, _NEG)
    s_wn = jnp.where(new_ok, lax.dot_general(q, kwn_ref[0], _NT, preferred_element_type=_F32), _NEG)
    m_w = jnp.maximum(jnp.max(s_w, -1, keepdims=True), jnp.max(s_wn, -1, keepdims=True))
    p_w = jnp.where(w_ok, jnp.exp(s_w - m_w), 0.0)
    p_wn = jnp.where(new_ok, jnp.exp(s_wn - m_w), 0.0)
    l_w = jnp.sum(p_w, -1, keepdims=True) + jnp.sum(p_wn, -1, keepdims=True)
    o_w = (lax.dot_general(p_w.astype(_BF16), vwc_ref[0].astype(_BF16), _NT, preferred_element_type=_F32)
           + jnp.dot(p_wn.astype(_BF16), vwn_ref[0], preferred_element_type=_F32)) / jnp.maximum(l_w, 1.0)

    k_sel = ksel_scr[slot].reshape(_NSA_KV, past).astype(_BF16)
    v_sel = vsel_scr[slot].reshape(_NSA_KV, past).astype(_BF16)
    s_s = (jnp.dot(q, k_sel, preferred_element_type=_F32)
           + jnp.dot(bias, e_ref[...], preferred_element_type=_F32))
    s_sn = jnp.where(new_ok, lax.dot_general(q, ksn_ref[0], _NT, preferred_element_type=_F32), _NEG)
    m_s = jnp.maximum(jnp.max(s_s, -1, keepdims=True), jnp.max(s_sn, -1, keepdims=True))
    p_s = jnp.exp(s_s - m_s)
    p_sn = jnp.where(new_ok, jnp.exp(s_sn - m_s), 0.0)
    l_s = jnp.sum(p_s, -1, keepdims=True) + jnp.sum(p_sn, -1, keepdims=True)
    o_s = (lax.dot_general(p_s.astype(_BF16), v_sel, _NT, preferred_element_type=_F32)
           + jnp.dot(p_sn.astype(_BF16), vsn_ref[0], preferred_element_type=_F32)) / jnp.maximum(l_s, 1.0)

    g = g_ref[0]
    o_ref[0] = g[:, 0:1] * o_c + g[:, 1:2] * o_s + g[:, 2:3] * o_w


def _nsa_sample(y, kcb, vcb, pool_k, pool_v, win_k, win_v, page_table, b, ts):
    n_pages = page_table.shape[1]
    past = n_pages * _PAGE
    n_past = past // _SEL_BLOCK
    assert ts <= _SEL_BLOCK and _SEL_TOPN - 1 <= n_past <= 128
    rows = _NSA_KVH * _NSA_G * ts
    col = lambda c, w: y[:, c:c + w]
    q = _block_diag_queries(col(_C_NQ, _NSA_Q).reshape(b, ts, _NSA_H, _NSA_DH))
    g = _gate_rows(col(_C_NG, 3 * _NSA_H).reshape(b, ts, _NSA_KVH, _NSA_G, 3))
    new = lambda c: col(c, _NSA_KV).astype(_BF16).reshape(b, ts, _NSA_KV)
    expand = (jnp.arange(past)[None, :] // _SEL_BLOCK == jnp.arange(n_past)[:, None]).astype(_BF16)
    per_b = lambda a: pl.BlockSpec((1,) + a.shape[1:], lambda i, pt, nd=a.ndim: (i,) + (0,) * (nd - 1))
    win_t = lambda a: _pool_token_minor(a).reshape(b, _NSA_KV, a.shape[1])
    blocked = [q, g, kcb.astype(_BF16), vcb.astype(_BF16), new(_C_KS), new(_C_VS),
               win_t(win_k), win_t(win_v), new(_C_KW), new(_C_VW)]
    pools = [_pool_token_minor(pool_k), _pool_token_minor(pool_v)]
    grid_spec = pltpu.PrefetchScalarGridSpec(
        num_scalar_prefetch=1,
        grid=(b,),
        in_specs=[per_b(a) for a in blocked] + [pl.BlockSpec(expand.shape, lambda i, pt: (0, 0))]
        + [pl.BlockSpec(memory_space=pl.ANY)] * 2,
        out_specs=pl.BlockSpec((1, rows, _NSA_KV), lambda i, pt: (i, 0, 0)),
        scratch_shapes=[pltpu.VMEM((2, _NSA_KVH, _NSA_DH, past), _F32),
                        pltpu.VMEM((2, _NSA_KVH, _NSA_DH, past), _F32),
                        pltpu.SemaphoreType.DMA((2, 2))],
    )
    o = pl.pallas_call(
        functools.partial(_nsa_sample_body, past=past, ts=ts),
        grid_spec=grid_spec,
        out_shape=jax.ShapeDtypeStruct((b, rows, _NSA_KV), _F32),
        compiler_params=_cparams(("arbitrary",)),
        name="nsa_sample",
    )(page_table, *blocked, expand, *pools)
    return _own_head_rows(o, ts).reshape(b * ts, _NSA_Q)


def _merge_body(o_ref, srg_ref, on_ref, mg0_ref, mg1_ref, x_ref, g1_ref, wbr_ref, wbn_ref, wout_ref,
                lng_ref, lnb_ref, out_ref):
    o = o_ref[...]
    parts = []
    for h in range(_RET_H):
        seg = o[:, h * _RET_DV:(h + 1) * _RET_DV]
        mu = jnp.mean(seg, -1, keepdims=True)
        d = seg - mu
        var = jnp.mean(d * d, -1, keepdims=True)
        parts.append(d * lax.rsqrt(var + _LN_EPS))
    a = (jnp.concatenate(parts, axis=1) * srg_ref[...]).astype(_BF16)
    br = jnp.dot(a, wbr_ref[...], preferred_element_type=_F32)
    bn = jnp.dot(on_ref[...].astype(_BF16), wbn_ref[...], preferred_element_type=_F32)
    m = (mg0_ref[...] * br + mg1_ref[...] * bn).astype(_BF16)
    yv = jnp.dot(m, wout_ref[...], preferred_element_type=_F32)
    r = _ALPHA * x_ref[...] + g1_ref[0] * yv
    out_ref[...] = _layer_norm(r, lng_ref[...], lnb_ref[...])


def _merge(o_r, y, o_n, x2, g13, mod_map, w_br_ret, w_br_nsa, w_out, ln_g, ln_b, tm):
    n = x2.shape[0]
    r = g13.shape[1]
    row = lambda c: pl.BlockSpec((tm, _D), lambda i: (i, c))
    wbr, wbn, wout = w_br_ret.astype(_BF16), w_br_nsa.astype(_BF16), w_out.astype(_BF16)
    lng, lnb = ln_g.reshape(1, _D), ln_b.reshape(1, _D)
    return pl.pallas_call(
        _merge_body,
        grid=(n // tm,),
        in_specs=[row(0), row(_C_RG // _D), pl.BlockSpec((tm, _NSA_Q), lambda i: (i, 0)),
                  row(_C_MG // _D), row(_C_MG // _D + 1), row(0),
                  pl.BlockSpec((1, r, _D), lambda i: (mod_map(i), 0, 0))]
        + [_full_spec(a, 1) for a in (wbr, wbn, wout, lng, lnb)],
        out_specs=row(0),
        out_shape=jax.ShapeDtypeStruct((n, _D), _F32),
        compiler_params=_cparams(("parallel",)),
        name="merge_ln1",
    )(o_r, y, o_n, y, y, x2, g13, wbr, wbn, wout, lng, lnb)


def _topk_rows(work, k):
    r = work.shape[0]
    iota = lax.broadcasted_iota(jnp.int32, work.shape, 0)
    vals = []
    for _ in range(k):
        mx = jnp.max(work, axis=0, keepdims=True)
        idx = jnp.min(jnp.where(work == mx, iota, r), axis=0, keepdims=True)
        work = jnp.where(iota == idx, -jnp.inf, work)
        vals.append(mx)
    return vals


def _peer_select_body(x_ref, sc_ref, sh_ref, wq_ref, sk_ref, xb_ref, st_ref, thr_ref, mz_ref, sv_scr):
    xb = (x_ref[...] * (1.0 + sc_ref[0]) + sh_ref[0]).astype(_BF16)
    xb_ref[...] = xb
    q = jnp.dot(xb, wq_ref[...], preferred_element_type=_F32).astype(_BF16)
    k = _PEER_TOPK
    neg = jnp.full((1, q.shape[0]), -jnp.inf, _F32)
    for h in range(_PEER_H):
        for p in range(2):
            hp = 2 * h + p
            st = lax.dot_general(sk_ref[hp], q[:, hp * _PEER_HALF:(hp + 1) * _PEER_HALF], _NT,
                                 preferred_element_type=_F32)
            st_ref[hp] = st
            for r, v in enumerate(_topk_rows(st, k)):
                sv_scr[p, r:r + 1, :] = v
        sv0, sv1 = sv_scr[0], sv_scr[1]
        row8 = lax.broadcasted_iota(jnp.int32, (8, 1), 0)
        row16 = lax.broadcasted_iota(jnp.int32, (k, 1), 0)
        groups = [sv0[0:1] + sv1, jnp.where(row16 == 0, neg, sv0 + sv1[0:1])]
        for a in range(1, 8):
            groups.append(jnp.where(row8 == 0, neg, sv0[a:a + 1] + sv1[0:8]))
        top = _topk_rows(jnp.concatenate(groups, axis=0), k)
        m = top[0]
        z = functools.reduce(lambda u, v: u + v, [jnp.exp(t - m) for t in top])
        thr_ref[h:h + 1, :] = top[k - 1]
        mz_ref[h:h + 1, :] = m + jnp.log(z)


def _peer_select(x1, sc3, sh3, mod_map, wq, subkeys, tn):
    n = x1.shape[0]
    r = sc3.shape[1]
    hp = 2 * _PEER_H
    wqb = wq.astype(_BF16)
    sk = subkeys.astype(_BF16).reshape(hp, _PEER_NK, _PEER_HALF)
    mod = pl.BlockSpec((1, r, _D), lambda i: (mod_map(i), 0, 0))
    return pl.pallas_call(
        _peer_select_body,
        grid=(n // tn,),
        in_specs=[pl.BlockSpec((tn, _D), lambda i: (i, 0)), mod, mod, _full_spec(wqb, 1), _full_spec(sk, 1)],
        out_specs=[pl.BlockSpec((tn, _D), lambda i: (i, 0)),
                   pl.BlockSpec((hp, _PEER_NK, tn), lambda i: (0, 0, i)),
                   pl.BlockSpec((_PEER_H, tn), lambda i: (0, i)),
                   pl.BlockSpec((_PEER_H, tn), lambda i: (0, i))],
        out_shape=[jax.ShapeDtypeStruct((n, _D), _BF16),
                   jax.ShapeDtypeStruct((hp, _PEER_NK, n), _F32),
                   jax.ShapeDtypeStruct((_PEER_H, n), _F32),
                   jax.ShapeDtypeStruct((_PEER_H, n), _F32)],
        scratch_shapes=[pltpu.VMEM((2, _PEER_TOPK, tn), _F32)],
        compiler_params=_cparams(("parallel",)),
        name="peer_select",
    )(x1, sc3, sh3, wqb, sk)


def _peer_main_body(xb_ref, u_ref, v_ref, st_ref, thr_ref, mz_ref, x_ref, g2_ref, lng_ref, lnb_ref,
                    out_ref, acc_ref, e_scr, g_scr):
    e = pl.program_id(1)
    slabs = u_ref.shape[0] // _PEER_NK
    n = xb_ref.shape[0]

    @pl.when(e == 0)
    def _():
        acc_ref[...] = jnp.zeros_like(acc_ref)
        for h in range(_PEER_H):
            s1 = st_ref[2 * h + 1]
            m1 = jnp.max(s1, axis=0, keepdims=True)
            e_scr[2 * h] = jnp.exp(st_ref[2 * h] - (mz_ref[h:h + 1, :] - m1))
            e_scr[2 * h + 1] = jnp.exp(s1 - m1)

    for sl in range(slabs):
        i = e * slabs + sl
        gt = jnp.zeros((_PEER_NK, n), _F32)
        for h in range(_PEER_H):
            c = st_ref[2 * h, pl.ds(i, 1), :] + st_ref[2 * h + 1]
            gt = jnp.where(c >= thr_ref[h:h + 1, :], gt + e_scr[2 * h, pl.ds(i, 1), :] * e_scr[2 * h + 1], gt)
        g_scr[sl * _PEER_NK:(sl + 1) * _PEER_NK, :] = gt
    act = _gelu(lax.dot_general(u_ref[...], xb_ref[...], _NT, preferred_element_type=_F32))
    wt = (act * g_scr[...]).astype(_BF16)
    acc_ref[...] += lax.dot_general(wt, v_ref[...], _TN_DIMS, preferred_element_type=_F32)

    @pl.when(e == pl.num_programs(1) - 1)
    def _():
        r = _ALPHA * x_ref[...] + g2_ref[0] * acc_ref[...]
        out_ref[...] = _layer_norm(r, lng_ref[...], lnb_ref[...])


def _peer_main(xb, u_b, v_b, st, thr, mz, x1, g23, mod_map, ln_g, ln_b, tn, ec):
    n = x1.shape[0]
    r = g23.shape[1]
    ne = u_b.shape[0]
    hp = 2 * _PEER_H
    lng, lnb = ln_g.reshape(1, _D), ln_b.reshape(1, _D)
    return pl.pallas_call(
        _peer_main_body,
        grid=(n // tn, ne // ec),
        in_specs=[pl.BlockSpec((tn, _D), lambda i, e: (i, 0)),
                  pl.BlockSpec((ec, _D), lambda i, e: (e, 0)),
                  pl.BlockSpec((ec, _D), lambda i, e: (e, 0)),
                  pl.BlockSpec((hp, _PEER_NK, tn), lambda i, e: (0, 0, i)),
                  pl.BlockSpec((_PEER_H, tn), lambda i, e: (0, i)),
                  pl.BlockSpec((_PEER_H, tn), lambda i, e: (0, i)),
                  pl.BlockSpec((tn, _D), lambda i, e: (i, 0)),
                  pl.BlockSpec((1, r, _D), lambda i, e: (mod_map(i), 0, 0)),
                  _full_spec(lng, 2), _full_spec(lnb, 2)],
        out_specs=pl.BlockSpec((tn, _D), lambda i, e: (i, 0)),
        out_shape=jax.ShapeDtypeStruct((n, _D), _F32),
        scratch_shapes=[pltpu.VMEM((tn, _D), _F32), pltpu.VMEM((hp, _PEER_NK, tn), _F32),
                        pltpu.VMEM((ec, tn), _F32)],
        compiler_params=_cparams(("parallel", "arbitrary")),
        name="peer_main",
    )(xb, u_b, v_b, st, thr, mz, x1, g23, lng, lnb)


_PEER_SEL_TOKENS = 256
_PEER_MAIN_TOKENS = 512
_PEER_EXPERT_CHUNK = 1024


def _peer_and_ln(x1, sc3, sh3, sel_map, g3, main_map, wq, subkeys, u_b, v_b, ln_g, ln_b):
    n = x1.shape[0]
    xb, st, thr, mz = _peer_select(x1, sc3, sh3, sel_map, wq, subkeys, min(_PEER_SEL_TOKENS, n))
    return _peer_main(xb, u_b, v_b, st, thr, mz, x1, g3, main_map, ln_g, ln_b, min(_PEER_MAIN_TOKENS, n),
                      _PEER_EXPERT_CHUNK)


def kernel(x_prompt, x_sample, c_prompt, c_sample, state_ret, cache_cmp_k, cache_cmp_v, cache_sel_k, cache_sel_v,
           cache_win_k, cache_win_v, page_table, w_ada, b_ada, w_in, cmp_pe_k, cmp_w1_k, cmp_w2_k, cmp_pe_v,
           cmp_w1_v, cmp_w2_v, w_br_ret, w_br_nsa, w_out, ln1_g, ln1_b, peer_wq, peer_subkeys, peer_u, peer_v,
           ln2_g, ln2_b):
    l = 0
    B, T, _ = x_prompt.shape
    Bs, Ts, _ = x_sample.shape
    n_pages = page_table.shape[1]
    past = n_pages * _PAGE
    wcat = _build_wcat(w_in[l])
    u_b, v_b = peer_u[l].astype(_BF16), peer_v[l].astype(_BF16)
    cw_k = _compress_weights(cmp_pe_k[l], cmp_w1_k[l], cmp_w2_k[l])
    cw_v = _compress_weights(cmp_pe_v[l], cmp_w1_v[l], cmp_w2_v[l])
    stride_w = _CMP_STRIDE * _NSA_KV
    cache5 = lambda a, b, t: a.reshape(b, t, _NSA_KVH, _NSA_DH)

    ns = Bs * Ts
    tms = min(512, ns)
    ada_s = _linear(c_sample, w_ada[l], b_ada[l])
    rowmod = lambda a, tile: jnp.broadcast_to(a[:, None, :], (Bs, Ts, _D)).reshape(ns // tile, tile, _D)
    mods = jnp.split(ada_s, 6, axis=-1)
    xs2 = x_sample.reshape(ns, _D)
    pos = past + jnp.arange(Ts, dtype=jnp.int32)
    tabs_s = _rope_tables(jnp.tile(pos, tms // Ts))
    ident = lambda i: i
    ys = _inproj(xs2, rowmod(mods[1], tms), rowmod(mods[0], tms), ident, wcat, tabs_s, lambda i: 0, tms)
    cols = lambda c, w: ys[:, c:c + w]
    assert Ts < _CMP_STRIDE
    kcb_s = _compress_paged(_pool_token_minor(cache_cmp_k[l]), page_table, cw_k).reshape(Bs, -1, _NSA_KV)
    vcb_s = _compress_paged(_pool_token_minor(cache_cmp_v[l]), page_table, cw_v).reshape(Bs, -1, _NSA_KV)
    o_ns = _nsa_sample(ys, kcb_s, vcb_s, cache_sel_k[l], cache_sel_v[l], cache_win_k[l], cache_win_v[l],
                       page_table, Bs, Ts)
    o_rs, r_new = _retention_sample(ys, state_ret[l], Bs, Ts)
    tms2 = min(256, ns)
    x1s = _merge(o_rs, ys, o_ns, xs2, rowmod(mods[2], tms2), ident, w_br_ret[l], w_br_nsa[l], w_out[l],
                 ln1_g[l], ln1_b[l], tms2)
    sel_t, main_t = min(_PEER_SEL_TOKENS, ns), min(_PEER_MAIN_TOKENS, ns)
    xs_out = _peer_and_ln(x1s, rowmod(mods[4], sel_t), rowmod(mods[3], sel_t), ident, rowmod(mods[5], main_t), ident,
                          peer_wq[l], peer_subkeys[l], u_b, v_b, ln2_g[l], ln2_b[l])
    new5 = lambda c: cache5(cols(c, _NSA_KV), Bs, Ts)
    kwin = jnp.concatenate([cache_win_k[l], new5(_C_KW)], axis=1)
    vwin = jnp.concatenate([cache_win_v[l], new5(_C_VW)], axis=1)
    sample_state = (r_new, new5(_C_KC), new5(_C_VC), new5(_C_KS), new5(_C_VS), kwin[:, Ts:], vwin[:, Ts:])

    tm = 512
    tpb = T // tm
    ada_p = _linear(c_prompt, w_ada[l], b_ada[l])
    sh1, sc1, g1, sh2, sc2, g2 = [a.reshape(B, 1, _D) for a in jnp.split(ada_p, 6, axis=-1)]
    xp2 = x_prompt.reshape(B * T, _D)
    tabs_p = _rope_tables(jnp.arange(T, dtype=jnp.int32))
    y = _inproj(xp2, sc1, sh1, lambda i: i // tpb, wcat, tabs_p, lambda i: i % tpb, tm)
    col = lambda c, w: y[:, c:c + w]
    o_r, R_p = _retention_prompt(y, B, T)
    nc = T // _CMP_STRIDE
    kcb = _compress_prompt(col(_C_KC, _NSA_KV).reshape(B * nc, stride_w), B, cw_k).reshape(B, nc, _NSA_KV)
    vcb = _compress_prompt(col(_C_VC, _NSA_KV).reshape(B * nc, stride_w), B, cw_v).reshape(B, nc, _NSA_KV)
    o_n = _nsa_prompt(y, kcb, vcb, B, T)
    tm2 = 256
    tpb2 = T // tm2
    bmap2 = lambda i: i // tpb2
    x1 = _merge(o_r, y, o_n, xp2, g1, bmap2, w_br_ret[l], w_br_nsa[l], w_out[l], ln1_g[l], ln1_b[l], tm2)
    xp_out = _peer_and_ln(x1, sc2, sh2, lambda i: i // (T // _PEER_SEL_TOKENS), g2,
                          lambda i: i // (T // _PEER_MAIN_TOKENS), peer_wq[l], peer_subkeys[l], u_b, v_b,
                          ln2_g[l], ln2_b[l])
    wb = min(_WINDOW, T)
    seq5 = lambda c: cache5(col(c, _NSA_KV), B, T)
    prompt_state = (R_p, seq5(_C_KC), seq5(_C_VC), seq5(_C_KS), seq5(_C_VS),
                    seq5(_C_KW)[:, T - wb:], seq5(_C_VW)[:, T - wb:])

    outs =[xp_out.reshape(B, T, _D), xs_out.reshape(Bs, Ts, _D)]
    outs += [a[None] for a in prompt_state]
    outs += [a[None] for a in sample_state]
    return tuple(outs)
```

```python
import functools
import math

import jax
import jax.numpy as jnp
import numpy as np
from jax import lax
from jax.experimental import pallas as pl
from jax.experimental.pallas import tpu as pltpu

_F32 = jnp.float32
_BF16 = jnp.bfloat16

_D = 1024
_RET_H, _RET_DK, _RET_DV, _RET_C = 4, 128, 256, 128
_NSA_H, _NSA_KVH, _NSA_G, _NSA_DH = 8, 2, 4, 64
_CMP_BLOCK, _CMP_STRIDE, _CMP_HID = 32, 16, 128
_SEL_BLOCK, _SEL_TOPN, _WINDOW, _QB = 64, 16, 512, 64
_FORCED = 1.0e4
_PEER_H, _PEER_NK, _PEER_HALF, _PEER_TOPK = 8, 128, 128, 16
_PAGE = 128
_ALPHA = 2.0 ** 0.25
_LN_EPS = 1e-5
_NEG = -1e30
_ROPE_THETA = 10000.0
_RET_QK = _RET_H * _RET_DK
_RET_V = _RET_H * _RET_DV
_NSA_Q = _NSA_H * _NSA_DH
_NSA_KV = _NSA_KVH * _NSA_DH
_IN_SECTIONS = (_RET_QK, _RET_QK, _RET_V, _RET_V, _NSA_Q, 6 * _NSA_KV, 3 * _NSA_H, 2 * _D)
_IN_SPLITS = tuple(int(s) for s in np.cumsum(_IN_SECTIONS)[:-1])
_RET_LOG_G = tuple(math.log1p(-2.0 ** (-5.0 - h)) for h in range(_RET_H))

_VMEM_LIMIT = 48 * 1024 * 1024
_INPROJ_VMEM_LIMIT = 56 * 1024 * 1024

_TN = 512
_C_RQ, _C_RK, _C_RV, _C_RG, _C_MG, _C_NQ = 0, 512, 1024, 2048, 3072, 5120
_C_KC, _C_KS, _C_KW = 5632, 5760, 5888
_C_VC, _C_VS, _C_VW, _C_NG = 6144, 6272, 6400, 6528
_IN_COLS = 13 * _TN

_SEL_CHUNK = 512
_NSA_ROWS = _NSA_KVH * _NSA_G * _QB

_NT = (((1,), (1,)), ((), ()))
_TN_DIMS = (((0,), (0,)), ((), ()))


def _cparams(sem):
    return pltpu.CompilerParams(dimension_semantics=sem, vmem_limit_bytes=_VMEM_LIMIT)


def _gelu(x):
    return 0.5 * x * (1.0 + lax.erf(x * (1.0 / math.sqrt(2.0))))


def _layer_norm(r, g, b):
    mu = jnp.mean(r, -1, keepdims=True)
    d = r - mu
    var = jnp.mean(d * d, -1, keepdims=True)
    return d * lax.rsqrt(var + _LN_EPS) * g + b


def _masked_softmax(s, mask):
    s = jnp.where(mask, s, _NEG)
    p = jnp.where(mask, jnp.exp(s - jnp.max(s, -1, keepdims=True)), 0.0)
    return p / jnp.maximum(jnp.sum(p, -1, keepdims=True), 1.0)


def _full_spec(a, grid_rank):
    zeros = (0,) * a.ndim
    return pl.BlockSpec(a.shape, {1: lambda i: zeros, 2: lambda i, j: zeros}[grid_rank])


def _linear_body(a_ref, w_ref, b_ref, o_ref):
    o_ref[...] = jnp.dot(a_ref[...].astype(_BF16), w_ref[...], preferred_element_type=_F32) + b_ref[...]


def _linear(a, w, b, tn=512):
    m, k = a.shape
    n = w.shape[1]
    return pl.pallas_call(
        _linear_body,
        grid=(n // tn,),
        in_specs=[pl.BlockSpec((m, k), lambda j: (0, 0)),
                  pl.BlockSpec((k, tn), lambda j: (0, j)),
                  pl.BlockSpec((1, tn), lambda j: (0, j))],
        out_specs=pl.BlockSpec((m, tn), lambda j: (0, j)),
        out_shape=jax.ShapeDtypeStruct((m, n), _F32),
        compiler_params=_cparams(("arbitrary",)),
        name="ada_linear",
    )(a, w.astype(_BF16), b.reshape(1, n))


def _build_wcat(w_in):
    rq, rk, rv, rg, nq, nkv, ng, mg = jnp.split(w_in, _IN_SPLITS, axis=1)
    nkv6 = nkv.reshape(_D, 6, _NSA_KV)
    kc, vc, ks, vs, kw, vw = [nkv6[:, i] for i in range(6)]
    z = jnp.zeros((_D, _NSA_KV), w_in.dtype)
    ngp = jnp.pad(ng, ((0, 0), (0, _NSA_KV - ng.shape[1])))
    wcat = jnp.concatenate([rq, rk, rv, rg, mg, nq, kc, ks, kw, z, vc, vs, vw, ngp], axis=1)
    return wcat.astype(_BF16)


def _rope_tables(pos):
    out = []
    for half in (_RET_DK // 2, _NSA_DH // 2):
        inv = _ROPE_THETA ** (-jnp.arange(half, dtype=_F32) / half)
        ang = pos.astype(_F32)[:, None] * inv[None, :]
        cos, sin = jnp.cos(ang), jnp.sin(ang)
        reps = _RET_DK // (2 * half)
        out.append(jnp.tile(jnp.concatenate([cos, cos], -1), (1, reps)))
        out.append(jnp.tile(jnp.concatenate([-sin, sin], -1), (1, reps)))
    return out


def _inproj_body(x_ref, sc_ref, sh_ref, w_ref, c128_ref, s128_ref, c64_ref, s64_ref, y_ref, *cache_t_ref):
    a = (x_ref[...] * (1.0 + sc_ref[0]) + sh_ref[0]).astype(_BF16)
    lane = lax.broadcasted_iota(jnp.int32, (a.shape[0], _TN), 1)
    reps = _TN // _RET_DK

    def rope(y, half, c_ref, s_ref):
        first = (lane % (2 * half)) < half
        partner = jnp.where(first, pltpu.roll(y, _TN - half, 1), pltpu.roll(y, half, 1))
        return y * jnp.tile(c_ref[...], (1, reps)) + partner * jnp.tile(s_ref[...], (1, reps))

    for j in range(_IN_COLS // _TN):
        c0 = j * _TN
        y = jnp.dot(a, w_ref[:, c0:c0 + _TN], preferred_element_type=_F32)
        if c0 == _C_RQ:
            y = rope(y, _RET_DK // 2, c128_ref, s128_ref)
        elif c0 == _C_RK:
            y = rope(y, _RET_DK // 2, c128_ref, s128_ref) * (_RET_DK ** -0.5)
        elif _C_RG <= c0 < _C_MG:
            y = y * jax.nn.sigmoid(y)
        elif _C_MG <= c0 < _C_NQ:
            y = jax.nn.sigmoid(y)
        elif _C_NQ <= c0 < _C_VC:
            y = rope(y, _NSA_DH // 2, c64_ref, s64_ref)
        elif c0 == _C_VC:
            y = jnp.where(lane < _C_NG - _C_VC, y, jax.nn.sigmoid(y))
        y_ref[:, c0:c0 + _TN] = y
        if cache_t_ref and c0 in (_C_KC, _C_VC):
            first = 0 if c0 == _C_KC else 3
            for sec in range(3):
                cache_t_ref[0][0, first + sec] = y[:, sec * _NSA_KV:(sec + 1) * _NSA_KV].T


def _inproj(x2, sc3, sh3, mod_map, wcat, tabs, tab_map, tm, cache_t_batches=None):
    n = x2.shape[0]
    r = sc3.shape[1]
    tab_spec = pl.BlockSpec((tm, _RET_DK), lambda i: (tab_map(i), 0))
    out_specs = pl.BlockSpec((tm, _IN_COLS), lambda i: (i, 0))
    out_shape = jax.ShapeDtypeStruct((n, _IN_COLS), _F32)
    if cache_t_batches:
        t = n // cache_t_batches
        tpb = t // tm
        out_specs = [out_specs, pl.BlockSpec((1, 6, _NSA_KV, tm), lambda i: (i // tpb, 0, 0, i % tpb))]
        out_shape = [out_shape, jax.ShapeDtypeStruct((cache_t_batches, 6, _NSA_KV, t), _F32)]
    return pl.pallas_call(
        _inproj_body,
        grid=(n // tm,),
        in_specs=[pl.BlockSpec((tm, _D), lambda i: (i, 0)),
                  pl.BlockSpec((1, r, _D), lambda i: (mod_map(i), 0, 0)),
                  pl.BlockSpec((1, r, _D), lambda i: (mod_map(i), 0, 0)),
                  pl.BlockSpec((_D, _IN_COLS), lambda i: (0, 0), pipeline_mode=pl.Buffered(1)),
                  tab_spec, tab_spec, tab_spec, tab_spec],
        out_specs=out_specs,
        out_shape=out_shape,
        compiler_params=pltpu.CompilerParams(dimension_semantics=("parallel",),
                                             vmem_limit_bytes=_INPROJ_VMEM_LIMIT),
        name="in_proj",
    )(x2, sc3, sh3, wcat, *tabs)


def _ret_consts(c):
    lg = jnp.asarray(_RET_LOG_G, _F32)[:, None, None]
    n = jnp.arange(c, dtype=_F32)
    diff = n[:, None] - n[None, :]
    decay = jnp.where(diff >= 0, jnp.exp(lg * jnp.maximum(diff, 0.0)), 0.0)
    cs = jnp.broadcast_to(jnp.exp(lg * (n[:, None] + 1.0)), (_RET_H, c, _RET_DV))
    kd = jnp.broadcast_to(jnp.exp(lg * (c - 1.0 - n[:, None])), (_RET_H, c, _RET_DK))
    return decay, cs, kd


def _ret_step(q_ref, k_ref, v_ref, dec_ref, cs_ref, kd_ref, o_ref, state, chunk):
    new = []
    for h in range(_RET_H):
        q = q_ref[:, h * _RET_DK:(h + 1) * _RET_DK].astype(_BF16)
        k = k_ref[:, h * _RET_DK:(h + 1) * _RET_DK]
        v = v_ref[:, h * _RET_DV:(h + 1) * _RET_DV].astype(_BF16)
        r = state(h)
        s = lax.dot_general(q, k.astype(_BF16), _NT, preferred_element_type=_F32) * dec_ref[h]
        inner = jnp.dot(s.astype(_BF16), v, preferred_element_type=_F32)
        cross = jnp.dot(q, r.astype(_BF16), preferred_element_type=_F32) * cs_ref[h]
        o_ref[:, h * _RET_DV:(h + 1) * _RET_DV] = inner + cross
        k_dec = (k * kd_ref[h]).astype(_BF16)
        new.append(r * math.exp(_RET_LOG_G[h] * chunk)
                   + lax.dot_general(k_dec, v, _TN_DIMS, preferred_element_type=_F32))
    return new


def _ret_prompt_body(q_ref, k_ref, v_ref, dec_ref, cs_ref, kd_ref, o_ref, r_ref):
    @pl.when(pl.program_id(1) == 0)
    def _():
        r_ref[...] = jnp.zeros_like(r_ref)

    new = _ret_step(q_ref, k_ref, v_ref, dec_ref, cs_ref, kd_ref, o_ref, lambda h: r_ref[0, h], _RET_C)
    for h in range(_RET_H):
        r_ref[0, h] = new[h]


def _retention_prompt(y, b, t):
    c = _RET_C
    nc = t // c
    consts = _ret_consts(c)
    return pl.pallas_call(
        _ret_prompt_body,
        grid=(b, nc),
        in_specs=[pl.BlockSpec((c, _RET_QK), lambda i, j: (i * nc + j, _C_RQ // _RET_QK)),
                  pl.BlockSpec((c, _RET_QK), lambda i, j: (i * nc + j, _C_RK // _RET_QK)),
                  pl.BlockSpec((c, _RET_V), lambda i, j: (i * nc + j, _C_RV // _RET_V))]
        + [_full_spec(a, 2) for a in consts],
        out_specs=[pl.BlockSpec((c, _RET_V), lambda i, j: (i * nc + j, 0)),
                   pl.BlockSpec((1, _RET_H, _RET_DK, _RET_DV), lambda i, j: (i, 0, 0, 0))],
        out_shape=[jax.ShapeDtypeStruct((b * t, _RET_V), _F32),
                   jax.ShapeDtypeStruct((b, _RET_H, _RET_DK, _RET_DV), _F32)],
        compiler_params=_cparams(("parallel", "arbitrary")),
        name="retention_prompt",
    )(y, y, y, *consts)


def _ret_sample_body(q_ref, k_ref, v_ref, st_ref, dec_ref, cs_ref, kd_ref, o_ref, r_ref, *, chunk):
    new = _ret_step(q_ref, k_ref, v_ref, dec_ref, cs_ref, kd_ref, o_ref, lambda h: st_ref[0, h], chunk)
    for h in range(_RET_H):
        r_ref[0, h] = new[h]


def _retention_sample(y, state, b, t):
    consts = _ret_consts(t)
    st_spec = pl.BlockSpec((1, _RET_H, _RET_DK, _RET_DV), lambda i: (i, 0, 0, 0))
    return pl.pallas_call(
        functools.partial(_ret_sample_body, chunk=t),
        grid=(b,),
        in_specs=[pl.BlockSpec((t, _RET_QK), lambda i: (i, _C_RQ // _RET_QK)),
                  pl.BlockSpec((t, _RET_QK), lambda i: (i, _C_RK // _RET_QK)),
                  pl.BlockSpec((t, _RET_V), lambda i: (i, _C_RV // _RET_V)),
                  st_spec] + [_full_spec(a, 1) for a in consts],
        out_specs=[pl.BlockSpec((t, _RET_V), lambda i: (i, 0)), st_spec],
        out_shape=[jax.ShapeDtypeStruct((b * t, _RET_V), _F32),
                   jax.ShapeDtypeStruct(state.shape, _F32)],
        compiler_params=_cparams(("parallel",)),
        name="retention_sample",
    )(y, y, y, state, *consts)


def _compress_weights(pe, w1, w2):
    w = w1.reshape(2, _CMP_STRIDE, _NSA_DH, _CMP_HID)
    eye = jnp.eye(_NSA_KVH, dtype=w1.dtype)
    wf = jnp.einsum('psde,kl->skdlpe', w, eye).reshape(_CMP_STRIDE * _NSA_KV, _NSA_KVH * 2 * _CMP_HID)
    pe8 = jnp.pad(pe.reshape(1, -1), ((0, 7), (0, 0)))
    return wf.astype(_BF16), pe8.astype(_BF16), w1.astype(_BF16), w2.astype(_BF16)


def _compress_compute(rows, wf_ref, pe_ref, w1_ref, w2_ref, o_ref, z_scr):
    ns = rows.shape[0]
    z = jnp.dot(rows.astype(_BF16), wf_ref[...], preferred_element_type=_F32)
    z_scr[pl.ds(0, ns), :] = z
    z_scr[pl.ds(ns, 8), :] = jnp.zeros((8, z.shape[1]), _F32)
    bias = jnp.dot(pe_ref[...], w1_ref[...], preferred_element_type=_F32)[0:1]
    for k in range(_NSA_KVH):
        c0 = k * 2 * _CMP_HID
        z_next = z_scr[pl.ds(1, ns), c0 + _CMP_HID:c0 + 2 * _CMP_HID]
        hid = _gelu(z[:, c0:c0 + _CMP_HID] + z_next + bias)
        o_ref[:, k * _NSA_DH:(k + 1) * _NSA_DH] = jnp.dot(hid.astype(_BF16), w2_ref[...],
                                                         preferred_element_type=_F32)


def _compress_body(rows_ref, wf_ref, pe_ref, w1_ref, w2_ref, o_ref, z_scr):
    _compress_compute(rows_ref[...], wf_ref, pe_ref, w1_ref, w2_ref, o_ref, z_scr)


def _compress_prompt(rows2, b, weights):
    ns = rows2.shape[0] // b
    return pl.pallas_call(
        _compress_body,
        grid=(b,),
        in_specs=[pl.BlockSpec((ns, rows2.shape[1]), lambda i: (i, 0))] + [_full_spec(a, 1) for a in weights],
        out_specs=pl.BlockSpec((ns, _NSA_KV), lambda i: (i, 0)),
        out_shape=jax.ShapeDtypeStruct((b * ns, _NSA_KV), _F32),
        scratch_shapes=[pltpu.VMEM((ns + 8, _NSA_KVH * 2 * _CMP_HID), _F32)],
        compiler_params=_cparams(("parallel",)),
        name="compress_prompt",
    )(rows2, *weights)


def _pool_token_minor(pool):
    return jnp.transpose(pool, (0, 2, 3, 1))


def _page_copies(pt_ref, b, pool_ref, dst_ref, sem):
    n_pages = pt_ref.shape[1]
    return [pltpu.make_async_copy(pool_ref.at[pt_ref[b, p]], dst_ref.at[:, :, pl.ds(p * _PAGE, _PAGE)], sem)
            for p in range(n_pages)]


def _prefetched_pages(copies_of):
    b = pl.program_id(0)
    slot = b % 2

    @pl.when(b == 0)
    def _():
        for cp in copies_of(b, slot):
            cp.start()

    @pl.when(b + 1 < pl.num_programs(0))
    def _():
        for cp in copies_of(b + 1, 1 - slot):
            cp.start()

    for cp in copies_of(b, slot):
        cp.wait()
    return slot


def _compress_paged_body(pt_ref, pool_ref, wf_ref, pe_ref, w1_ref, w2_ref, o_ref, xt_scr, x_scr, rows_scr, z_scr,
                         sem):
    slot = _prefetched_pages(lambda b, s: _page_copies(pt_ref, b, pool_ref, xt_scr.at[s], sem.at[s]))
    past = xt_scr.shape[3]
    ns = past // _CMP_STRIDE
    x_scr[...] = xt_scr[slot].reshape(_NSA_KV, past).T
    for s in range(_CMP_STRIDE):
        rows_scr[:, s * _NSA_KV:(s + 1) * _NSA_KV] = x_scr[pl.ds(s, ns, stride=_CMP_STRIDE), :]
    _compress_compute(rows_scr[...], wf_ref, pe_ref, w1_ref, w2_ref, o_ref, z_scr)


def _compress_paged(pool_t, page_table, weights):
    b, n_pages = page_table.shape
    past = n_pages * _PAGE
    ns = past // _CMP_STRIDE
    grid_spec = pltpu.PrefetchScalarGridSpec(
        num_scalar_prefetch=1,
        grid=(b,),
        in_specs=[pl.BlockSpec(memory_space=pl.ANY)]
        + [pl.BlockSpec(a.shape, lambda i, pt, nd=a.ndim: (0,) * nd) for a in weights],
        out_specs=pl.BlockSpec((ns, _NSA_KV), lambda i, pt: (i, 0)),
        scratch_shapes=[pltpu.VMEM((2, _NSA_KVH, _NSA_DH, past), _F32),
                        pltpu.VMEM((past, _NSA_KV), _F32),
                        pltpu.VMEM((ns, _CMP_STRIDE * _NSA_KV), _F32),
                        pltpu.VMEM((ns + 8, _NSA_KVH * 2 * _CMP_HID), _F32),
                        pltpu.SemaphoreType.DMA((2,))],
    )
    return pl.pallas_call(
        _compress_paged_body,
        grid_spec=grid_spec,
        out_shape=jax.ShapeDtypeStruct((b * ns, _NSA_KV), _F32),
        compiler_params=_cparams(("arbitrary",)),
        name="compress_paged",
    )(page_table, pool_t, *weights)


def _overlap_matrix(n_cmp, n_sel_pad):
    nn = lax.broadcasted_iota(jnp.int32, (n_cmp, n_sel_pad), 0)
    jj = lax.broadcasted_iota(jnp.int32, (n_cmp, n_sel_pad), 1)
    per = _SEL_BLOCK // _CMP_STRIDE
    return ((nn < (jj + 1) * per) & (nn * _CMP_STRIDE + _CMP_BLOCK > jj * _SEL_BLOCK)).astype(_BF16)


def _importance(pcs, ov):
    hi = pcs.astype(_BF16)
    lo = (pcs - hi.astype(_F32)).astype(_BF16)
    return jnp.dot(hi, ov, preferred_element_type=_F32) + jnp.dot(lo, ov, preferred_element_type=_F32)


def _topk_mask_rows(work, k):
    r = work.shape[0]
    iota = lax.broadcasted_iota(jnp.int32, work.shape, 0)
    sel = jnp.zeros(work.shape, _F32)
    for _ in range(k):
        mx = jnp.max(work, axis=0, keepdims=True)
        idx = jnp.min(jnp.where(work == mx, iota, r), axis=0, keepdims=True)
        pick = iota == idx
        sel = jnp.where(pick, 1.0, sel)
        work = jnp.where(pick, -jnp.inf, work)
    return sel


def _nsa_prompt_body(q_ref, g_ref, kc_ref, vc_ref, ks_ref, vs_ref, kw_ref, vw_ref, o_ref, *, seq):
    qb = pl.program_id(1)
    rows = _NSA_ROWS
    n_sel_pad = ks_ref.shape[3] - _NSA_KV
    q = q_ref[0, 0]
    row = lax.broadcasted_iota(jnp.int32, (rows, 1), 0)
    tpos = qb * _QB + row % _QB

    n_cmp = kc_ref.shape[1]
    s_c = lax.dot_general(q, kc_ref[0], _NT, preferred_element_type=_F32)
    n_i = lax.broadcasted_iota(jnp.int32, (1, n_cmp), 1)
    p_c = _masked_softmax(s_c, (n_i * _CMP_STRIDE + _CMP_BLOCK - 1) <= tpos)
    o_c = jnp.dot(p_c.astype(_BF16), vc_ref[0], preferred_element_type=_F32)

    gq = _NSA_G * _QB
    pcs = jnp.concatenate(
        [functools.reduce(lambda u, v: u + v, [p_c[k * gq + g * _QB:k * gq + (g + 1) * _QB] for g in range(_NSA_G)])
         for k in range(_NSA_KVH)], axis=0)
    imp = _importance(pcs, _overlap_matrix(n_cmp, n_sel_pad))
    j_i = lax.broadcasted_iota(jnp.int32, imp.shape, 1)
    forced = (j_i == 0) | (j_i == qb) | (j_i == qb - 1)
    score = jnp.where(j_i <= qb, jnp.where(forced, _FORCED, imp), -1.0)
    score_t = score.T
    sel = (_topk_mask_rows(score_t, _SEL_TOPN) * (score_t >= 0.0).astype(_F32)).T
    bias = jnp.where(sel > 0.5, 0.0, _NEG).astype(_BF16)
    bias = jnp.concatenate([bias[k * _QB:(k + 1) * _QB] for k in range(_NSA_KVH) for _ in range(_NSA_G)], axis=0)

    q_sel = jnp.concatenate([q, bias], axis=1)
    k_i = lax.broadcasted_iota(jnp.int32, (1, _SEL_CHUNK), 1)

    def chunk_step(c, carry, causal):
        m, l, acc = carry
        s = lax.dot_general(q_sel, ks_ref[0, c], _NT, preferred_element_type=_F32)
        if causal:
            s = jnp.where((c * _SEL_CHUNK + k_i) <= tpos, s, _NEG)
        m_new = jnp.maximum(m, jnp.max(s, -1, keepdims=True))
        p = jnp.exp(s - m_new)
        alpha = jnp.exp(m - m_new)
        l = alpha * l + jnp.sum(p, -1, keepdims=True)
        acc = alpha * acc + jnp.dot(p.astype(_BF16), vs_ref[0, c], preferred_element_type=_F32)
        return m_new, l, acc

    init = (jnp.full((rows, 1), _NEG, _F32), jnp.zeros((rows, 1), _F32), jnp.zeros((rows, _NSA_KV), _F32))
    last = qb // (_SEL_CHUNK // _SEL_BLOCK)
    carry = lax.fori_loop(0, last, lambda c, cr: chunk_step(c, cr, False), init)
    _, l_s, acc_s = chunk_step(last, carry, True)
    o_s = acc_s / jnp.maximum(l_s, 1.0)

    wlen = _WINDOW + 2 * _QB
    w0 = pl.multiple_of(jnp.clip(qb * _QB - (_WINDOW + _QB), 0, seq - wlen), _QB)
    s_w = lax.dot_general(q, kw_ref[0, pl.ds(w0, wlen), :], _NT, preferred_element_type=_F32)
    kpos = w0 + lax.broadcasted_iota(jnp.int32, (1, wlen), 1)
    p_w = _masked_softmax(s_w, (kpos <= tpos) & (kpos > tpos - _WINDOW))
    o_w = jnp.dot(p_w.astype(_BF16), vw_ref[0, pl.ds(w0, wlen), :], preferred_element_type=_F32)

    g = g_ref[0, 0]
    o_ref[0, 0] = g[:, 0:1] * o_c + g[:, 1:2] * o_s + g[:, 2:3] * o_w


def _block_diag_queries(nq):
    *lead, t, _, _ = nq.shape
    q = (nq * (_NSA_DH ** -0.5)).astype(_BF16).reshape(*lead, t, _NSA_KVH, _NSA_G, _NSA_DH)
    q = jnp.moveaxis(q, -4, -2)
    eye = jnp.eye(_NSA_KVH, dtype=_BF16)
    q = q[..., None, :] * eye[:, None, None, :, None]
    return q.reshape(*lead, _NSA_KVH * _NSA_G * t, _NSA_KV)


def _gate_rows(ng):
    *lead, t, _, _, _ = ng.shape
    return jnp.moveaxis(ng, -4, -2).reshape(*lead, _NSA_KVH * _NSA_G * t, 3)


def _own_head_rows(o, t):
    lead = o.shape[:-2]
    o = o.reshape(*lead, _NSA_KVH, _NSA_G, t, _NSA_KVH, _NSA_DH)
    o = jnp.stack([o[..., k, :, :, k, :] for k in range(_NSA_KVH)], axis=-4)
    return jnp.moveaxis(o, -2, -4).reshape(*lead, t, _NSA_Q)


def _nsa_prompt(y, kcb, vcb, b, t):
    nqb = t // _QB
    n_sel_pad = 128
    assert nqb <= n_sel_pad and t % _SEL_CHUNK == 0
    col = lambda c, w: y[:, c:c + w]
    q = _block_diag_queries(col(_C_NQ, _NSA_Q).reshape(b, nqb, _QB, _NSA_H, _NSA_DH))
    g = _gate_rows(col(_C_NG, 3 * _NSA_H).reshape(b, nqb, _QB, _NSA_KVH, _NSA_G, 3))
    seq_b = lambda c: col(c, _NSA_KV).astype(_BF16).reshape(b, t, _NSA_KV)
    chunked = lambda a: a.reshape(b, t // _SEL_CHUNK, _SEL_CHUNK, a.shape[-1])
    block_id = (jnp.arange(t)[:, None] // _SEL_BLOCK == jnp.arange(n_sel_pad)[None, :]).astype(_BF16)
    ks = chunked(jnp.concatenate([seq_b(_C_KS), jnp.broadcast_to(block_id, (b, t, n_sel_pad))], axis=-1))
    vs, kw, vw = chunked(seq_b(_C_VS)), seq_b(_C_KW), seq_b(_C_VW)
    kc, vc = kcb.astype(_BF16), vcb.astype(_BF16)
    per_q = lambda last: pl.BlockSpec((1, 1, _NSA_ROWS, last), lambda i, j: (i, j, 0, 0))
    per_b = lambda a: pl.BlockSpec((1,) + a.shape[1:], lambda i, j: (i,) + (0,) * (a.ndim - 1))
    o = pl.pallas_call(
        functools.partial(_nsa_prompt_body, seq=t),
        grid=(b, nqb),
        in_specs=[per_q(_NSA_KV), per_q(3), per_b(kc), per_b(vc), per_b(ks), per_b(vs), per_b(kw), per_b(vw)],
        out_specs=per_q(_NSA_KV),
        out_shape=jax.ShapeDtypeStruct((b, nqb, _NSA_ROWS, _NSA_KV), _F32),
        compiler_params=_cparams(("parallel", "arbitrary")),
        name="nsa_prompt",
    )(q, g, kc, vc, ks, vs, kw, vw)
    return _own_head_rows(o, _QB).reshape(b * t, _NSA_Q)


def _nsa_sample_body(pt_ref, q_ref, g_ref, kc_ref, vc_ref, ksn_ref, vsn_ref, kwc_ref, vwc_ref, kwn_ref, vwn_ref,
                     e_ref, poolk_ref, poolv_ref, o_ref, ksel_scr, vsel_scr, sem, *, past, ts):
    slot = _prefetched_pages(lambda b, s: (_page_copies(pt_ref, b, poolk_ref, ksel_scr.at[s], sem.at[0, s])
                                           + _page_copies(pt_ref, b, poolv_ref, vsel_scr.at[s], sem.at[1, s])))

    rows = _NSA_KVH * _NSA_G * ts
    q = q_ref[0]
    row = lax.broadcasted_iota(jnp.int32, (rows, 1), 0)
    t_new = row % ts
    tpos = past + t_new
    new_ok = lax.broadcasted_iota(jnp.int32, (1, ts), 1) <= t_new

    n_cmp = kc_ref.shape[1]
    s_c = lax.dot_general(q, kc_ref[0], _NT, preferred_element_type=_F32)
    n_i = lax.broadcasted_iota(jnp.int32, (1, n_cmp), 1)
    p_c = _masked_softmax(s_c, (n_i * _CMP_STRIDE + _CMP_BLOCK - 1) <= tpos)
    o_c = jnp.dot(p_c.astype(_BF16), vc_ref[0], preferred_element_type=_F32)

    n_past = e_ref.shape[0]
    gt = _NSA_G * ts
    pcs = jnp.concatenate(
        [functools.reduce(lambda u, v: u + v, [p_c[k * gt + g * ts:k * gt + (g + 1) * ts] for g in range(_NSA_G)])
         for k in range(_NSA_KVH)], axis=0)
    imp = _importance(pcs, _overlap_matrix(n_cmp, n_past))
    j_i = lax.broadcasted_iota(jnp.int32, imp.shape, 1)
    cur = past // _SEL_BLOCK
    score = jnp.where((j_i == 0) | (j_i == cur - 1), _FORCED, imp)
    sel = _topk_mask_rows(score.T, _SEL_TOPN - 1).T
    bias = jnp.where(sel > 0.5, 0.0, _NEG).astype(_BF16)
    bias = jnp.concatenate([bias[k * ts:(k + 1) * ts] for k in range(_NSA_KVH) for _ in range(_NSA_G)], axis=0)

    wb = kwc_ref.shape[2]
    s_w = jnp.dot(q, kwc_ref[0].astype(_BF16), preferred_element_type=_F32)
    w_i = lax.broadcasted_iota(jnp.int32, (1, wb), 1)
    w_ok = (past - wb + w_i) > (tpos - _WINDOW)
    s_w = jnp.where(w_ok, s_w, _NEG)
    s_wn = jnp.where(new_ok, lax.dot_general(q, kwn_ref[0], _NT, preferred_element_type=_F32), _NEG)
    m_w = jnp.maximum(jnp.max(s_w, -1, keepdims=True), jnp.max(s_wn, -1, keepdims=True))
    p_w = jnp.where(w_ok, jnp.exp(s_w - m_w), 0.0)
    p_wn = jnp.where(new_ok, jnp.exp(s_wn - m_w), 0.0)
    l_w = jnp.sum(p_w, -1, keepdims=True) + jnp.sum(p_wn, -1, keepdims=True)
    o_w = (lax.dot_general(p_w.astype(_BF16), vwc_ref[0].astype(_BF16), _NT, preferred_element_type=_F32)
           + jnp.dot(p_wn.astype(_BF16), vwn_ref[0], preferred_element_type=_F32)) / jnp.maximum(l_w, 1.0)

    k_sel = ksel_scr[slot].reshape(_NSA_KV, past).astype(_BF16)
    v_sel = vsel_scr[slot].reshape(_NSA_KV, past).astype(_BF16)
    s_s = (jnp.dot(q, k_sel, preferred_element_type=_F32)
           + jnp.dot(bias, e_ref[...], preferred_element_type=_F32))
    s_sn = jnp.where(new_ok, lax.dot_general(q, ksn_ref[0], _NT, preferred_element_type=_F32), _NEG)
    m_s = jnp.maximum(jnp.max(s_s, -1, keepdims=True), jnp.max(s_sn, -1, keepdims=True))
    p_s = jnp.exp(s_s - m_s)
    p_sn = jnp.where(new_ok, jnp.exp(s_sn - m_s), 0.0)
    l_s = jnp.sum(p_s, -1, keepdims=True) + jnp.sum(p_sn, -1, keepdims=True)
    o_s = (lax.dot_general(p_s.astype(_BF16), v_sel, _NT, preferred_element_type=_F32)
           + jnp.dot(p_sn.astype(_BF16), vsn_ref[0], preferred_element_type=_F32)) / jnp.maximum(l_s, 1.0)

    g = g_ref[0]
    o_ref[0] = g[:, 0:1] * o_c + g[:, 1:2] * o_s + g[:, 2:3] * o_w


def _nsa_sample(y, kcb, vcb, pool_k, pool_v, win_k, win_v, page_table, b, ts):
    n_pages = page_table.shape[1]
    past = n_pages * _PAGE
    n_past = past // _SEL_BLOCK
    assert ts <= _SEL_BLOCK and _SEL_TOPN - 1 <= n_past <= 128
    rows = _NSA_KVH * _NSA_G * ts
    col = lambda c, w: y[:, c:c + w]
    q = _block_diag_queries(col(_C_NQ, _NSA_Q).reshape(b, ts, _NSA_H, _NSA_DH))
    g = _gate_rows(col(_C_NG, 3 * _NSA_H).reshape(b, ts, _NSA_KVH, _NSA_G, 3))
    new = lambda c: col(c, _NSA_KV).astype(_BF16).reshape(b, ts, _NSA_KV)
    expand = (jnp.arange(past)[None, :] // _SEL_BLOCK == jnp.arange(n_past)[:, None]).astype(_BF16)
    per_b = lambda a: pl.BlockSpec((1,) + a.shape[1:], lambda i, pt, nd=a.ndim: (i,) + (0,) * (nd - 1))
    win_t = lambda a: _pool_token_minor(a).reshape(b, _NSA_KV, a.shape[1])
    blocked = [q, g, kcb.astype(_BF16), vcb.astype(_BF16), new(_C_KS), new(_C_VS),
               win_t(win_k), win_t(win_v), new(_C_KW), new(_C_VW)]
    pools = [_pool_token_minor(pool_k), _pool_token_minor(pool_v)]
    grid_spec = pltpu.PrefetchScalarGridSpec(
        num_scalar_prefetch=1,
        grid=(b,),
        in_specs=[per_b(a) for a in blocked] + [pl.BlockSpec(expand.shape, lambda i, pt: (0, 0))]
        + [pl.BlockSpec(memory_space=pl.ANY)] * 2,
        out_specs=pl.BlockSpec((1, rows, _NSA_KV), lambda i, pt: (i, 0, 0)),
        scratch_shapes=[pltpu.VMEM((2, _NSA_KVH, _NSA_DH, past), _F32),
                        pltpu.VMEM((2, _NSA_KVH, _NSA_DH, past), _F32),
                        pltpu.SemaphoreType.DMA((2, 2))],
    )
    o = pl.pallas_call(
        functools.partial(_nsa_sample_body, past=past, ts=ts),
        grid_spec=grid_spec,
        out_shape=jax.ShapeDtypeStruct((b, rows, _NSA_KV), _F32),
        compiler_params=_cparams(("arbitrary",)),
        name="nsa_sample",
    )(page_table, *blocked, expand, *pools)
    return _own_head_rows(o, ts).reshape(b * ts, _NSA_Q)


def _merge_body(o_ref, srg_ref, on_ref, mg0_ref, mg1_ref, x_ref, g1_ref, wbr_ref, wbn_ref, wout_ref,
                lng_ref, lnb_ref, out_ref):
    o = o_ref[...]
    parts = []
    for h in range(_RET_H):
        seg = o[:, h * _RET_DV:(h + 1) * _RET_DV]
        mu = jnp.mean(seg, -1, keepdims=True)
        d = seg - mu
        var = jnp.mean(d * d, -1, keepdims=True)
        parts.append(d * lax.rsqrt(var + _LN_EPS))
    a = (jnp.concatenate(parts, axis=1) * srg_ref[...]).astype(_BF16)
    br = jnp.dot(a, wbr_ref[...], preferred_element_type=_F32)
    bn = jnp.dot(on_ref[...].astype(_BF16), wbn_ref[...], preferred_element_type=_F32)
    m = (mg0_ref[...] * br + mg1_ref[...] * bn).astype(_BF16)
    yv = jnp.dot(m, wout_ref[...], preferred_element_type=_F32)
    r = _ALPHA * x_ref[...] + g1_ref[0] * yv
    out_ref[...] = _layer_norm(r, lng_ref[...], lnb_ref[...])


def _merge(o_r, y, o_n, x2, g13, mod_map, w_br_ret, w_br_nsa, w_out, ln_g, ln_b, tm):
    n = x2.shape[0]
    r = g13.shape[1]
    row = lambda c: pl.BlockSpec((tm, _D), lambda i: (i, c))
    wbr, wbn, wout = w_br_ret.astype(_BF16), w_br_nsa.astype(_BF16), w_out.astype(_BF16)
    lng, lnb = ln_g.reshape(1, _D), ln_b.reshape(1, _D)
    return pl.pallas_call(
        _merge_body,
        grid=(n // tm,),
        in_specs=[row(0), row(_C_RG // _D), pl.BlockSpec((tm, _NSA_Q), lambda i: (i, 0)),
                  row(_C_MG // _D), row(_C_MG // _D + 1), row(0),
                  pl.BlockSpec((1, r, _D), lambda i: (mod_map(i), 0, 0))]
        + [_full_spec(a, 1) for a in (wbr, wbn, wout, lng, lnb)],
        out_specs=row(0),
        out_shape=jax.ShapeDtypeStruct((n, _D), _F32),
        compiler_params=_cparams(("parallel",)),
        name="merge_ln1",
    )(o_r, y, o_n, y, y, x2, g13, wbr, wbn, wout, lng, lnb)


def _topk_rows(work, k):
    r = work.shape[0]
    iota = lax.broadcasted_iota(jnp.int32, work.shape, 0)
    vals = []
    for _ in range(k):
        mx = jnp.max(work, axis=0, keepdims=True)
        idx = jnp.min(jnp.where(work == mx, iota, r), axis=0, keepdims=True)
        work = jnp.where(iota == idx, -jnp.inf, work)
        vals.append(mx)
    return vals


def _peer_select_body(x_ref, sc_ref, sh_ref, wq_ref, sk_ref, xb_ref, st_ref, thr_ref, mz_ref, sv_scr):
    xb = (x_ref[...] * (1.0 + sc_ref[0]) + sh_ref[0]).astype(_BF16)
    xb_ref[...] = xb
    q = jnp.dot(xb, wq_ref[...], preferred_element_type=_F32).astype(_BF16)
    k = _PEER_TOPK
    neg = jnp.full((1, q.shape[0]), -jnp.inf, _F32)
    for h in range(_PEER_H):
        for p in range(2):
            hp = 2 * h + p
            st = lax.dot_general(sk_ref[hp], q[:, hp * _PEER_HALF:(hp + 1) * _PEER_HALF], _NT,
                                 preferred_element_type=_F32)
            st_ref[hp] = st
            for r, v in enumerate(_topk_rows(st, k)):
                sv_scr[p, r:r + 1, :] = v
        sv0, sv1 = sv_scr[0], sv_scr[1]
        row8 = lax.broadcasted_iota(jnp.int32, (8, 1), 0)
        row16 = lax.broadcasted_iota(jnp.int32, (k, 1), 0)
        groups = [sv0[0:1] + sv1, jnp.where(row16 == 0, neg, sv0 + sv1[0:1])]
        for a in range(1, 8):
            groups.append(jnp.where(row8 == 0, neg, sv0[a:a + 1] + sv1[0:8]))
        top = _topk_rows(jnp.concatenate(groups, axis=0), k)
        m = top[0]
        z = functools.reduce(lambda u, v: u + v, [jnp.exp(t - m) for t in top])
        thr_ref[h:h + 1, :] = top[k - 1]
        mz_ref[h:h + 1, :] = m + jnp.log(z)


def _peer_select(x1, sc3, sh3, mod_map, wq, subkeys, tn):
    n = x1.shape[0]
    r = sc3.shape[1]
    hp = 2 * _PEER_H
    wqb = wq.astype(_BF16)
    sk = subkeys.astype(_BF16).reshape(hp, _PEER_NK, _PEER_HALF)
    mod = pl.BlockSpec((1, r, _D), lambda i: (mod_map(i), 0, 0))
    return pl.pallas_call(
        _peer_select_body,
        grid=(n // tn,),
        in_specs=[pl.BlockSpec((tn, _D), lambda i: (i, 0)), mod, mod, _full_spec(wqb, 1), _full_spec(sk, 1)],
        out_specs=[pl.BlockSpec((tn, _D), lambda i: (i, 0)),
                   pl.BlockSpec((hp, _PEER_NK, tn), lambda i: (0, 0, i)),
                   pl.BlockSpec((_PEER_H, tn), lambda i: (0, i)),
                   pl.BlockSpec((_PEER_H, tn), lambda i: (0, i))],
        out_shape=[jax.ShapeDtypeStruct((n, _D), _BF16),
                   jax.ShapeDtypeStruct((hp, _PEER_NK, n), _F32),
                   jax.ShapeDtypeStruct((_PEER_H, n), _F32),
                   jax.ShapeDtypeStruct((_PEER_H, n), _F32)],
        scratch_shapes=[pltpu.VMEM((2, _PEER_TOPK, tn), _F32)],
        compiler_params=_cparams(("parallel",)),
        name="peer_select",
    )(x1, sc3, sh3, wqb, sk)


def _peer_main_body(xb_ref, u_ref, v_ref, st_ref, thr_ref, mz_ref, x_ref, g2_ref, lng_ref, lnb_ref,
                    out_ref, acc_ref, e_scr, g_scr):
    e = pl.program_id(1)
    slabs = u_ref.shape[0] // _PEER_NK
    n = xb_ref.shape[0]

    @pl.when(e == 0)
    def _():
        acc_ref[...] = jnp.zeros_like(acc_ref)
        for h in range(_PEER_H):
            s1 = st_ref[2 * h + 1]
            m1 = jnp.max(s1, axis=0, keepdims=True)
            e_scr[2 * h] = jnp.exp(st_ref[2 * h] - (mz_ref[h:h + 1, :] - m1))
            e_scr[2 * h + 1] = jnp.exp(s1 - m1)

    for sl in range(slabs):
        i = e * slabs + sl
        gt = jnp.zeros((_PEER_NK, n), _F32)
        for h in range(_PEER_H):
            c = st_ref[2 * h, pl.ds(i, 1), :] + st_ref[2 * h + 1]
            gt = jnp.where(c >= thr_ref[h:h + 1, :], gt + e_scr[2 * h, pl.ds(i, 1), :] * e_scr[2 * h + 1], gt)
        g_scr[sl * _PEER_NK:(sl + 1) * _PEER_NK, :] = gt
    act = _gelu(lax.dot_general(u_ref[...], xb_ref[...], _NT, preferred_element_type=_F32))
    wt = (act * g_scr[...]).astype(_BF16)
    acc_ref[...] += lax.dot_general(wt, v_ref[...], _TN_DIMS, preferred_element_type=_F32)

    @pl.when(e == pl.num_programs(1) - 1)
    def _():
        r = _ALPHA * x_ref[...] + g2_ref[0] * acc_ref[...]
        out_ref[...] = _layer_norm(r, lng_ref[...], lnb_ref[...])


def _peer_main(xb, u_b, v_b, st, thr, mz, x1, g23, mod_map, ln_g, ln_b, tn, ec):
    n = x1.shape[0]
    r = g23.shape[1]
    ne = u_b.shape[0]
    hp = 2 * _PEER_H
    lng, lnb = ln_g.reshape(1, _D), ln_b.reshape(1, _D)
    return pl.pallas_call(
        _peer_main_body,
        grid=(n // tn, ne // ec),
        in_specs=[pl.BlockSpec((tn, _D), lambda i, e: (i, 0)),
                  pl.BlockSpec((ec, _D), lambda i, e: (e, 0)),
                  pl.BlockSpec((ec, _D), lambda i, e: (e, 0)),
                  pl.BlockSpec((hp, _PEER_NK, tn), lambda i, e: (0, 0, i)),
                  pl.BlockSpec((_PEER_H, tn), lambda i, e: (0, i)),
                  pl.BlockSpec((_PEER_H, tn), lambda i, e: (0, i)),
                  pl.BlockSpec((tn, _D), lambda i, e: (i, 0)),
                  pl.BlockSpec((1, r, _D), lambda i, e: (mod_map(i), 0, 0)),
                  _full_spec(lng, 2), _full_spec(lnb, 2)],
        out_specs=pl.BlockSpec((tn, _D), lambda i, e: (i, 0)),
        out_shape=jax.ShapeDtypeStruct((n, _D), _F32),
        scratch_shapes=[pltpu.VMEM((tn, _D), _F32), pltpu.VMEM((hp, _PEER_NK, tn), _F32),
                        pltpu.VMEM((ec, tn), _F32)],
        compiler_params=_cparams(("parallel", "arbitrary")),
        name="peer_main",
    )(xb, u_b, v_b, st, thr, mz, x1, g23, lng, lnb)


_PEER_SEL_TOKENS = 256
_PEER_MAIN_TOKENS = 512
_PEER_EXPERT_CHUNK = 1024


def _peer_and_ln(x1, sc3, sh3, sel_map, g3, main_map, wq, subkeys, u_b, v_b, ln_g, ln_b):
    n = x1.shape[0]
    xb, st, thr, mz = _peer_select(x1, sc3, sh3, sel_map, wq, subkeys, min(_PEER_SEL_TOKENS, n))
    return _peer_main(xb, u_b, v_b, st, thr, mz, x1, g3, main_map, ln_g, ln_b, min(_PEER_MAIN_TOKENS, n),
                      _PEER_EXPERT_CHUNK)


def kernel(x_prompt, x_sample, c_prompt, c_sample, state_ret, cache_cmp_k, cache_cmp_v, cache_sel_k, cache_sel_v,
           cache_win_k, cache_win_v, page_table, w_ada, b_ada, w_in, cmp_pe_k, cmp_w1_k, cmp_w2_k, cmp_pe_v,
           cmp_w1_v, cmp_w2_v, w_br_ret, w_br_nsa, w_out, ln1_g, ln1_b, peer_wq, peer_subkeys, peer_u, peer_v,
           ln2_g, ln2_b):
    l = 0
    B, T, _ = x_prompt.shape
    Bs, Ts, _ = x_sample.shape
    n_pages = page_table.shape[1]
    past = n_pages * _PAGE
    wcat = _build_wcat(w_in[l])
    u_b, v_b = peer_u[l].astype(_BF16), peer_v[l].astype(_BF16)
    cw_k = _compress_weights(cmp_pe_k[l], cmp_w1_k[l], cmp_w2_k[l])
    cw_v = _compress_weights(cmp_pe_v[l], cmp_w1_v[l], cmp_w2_v[l])
    stride_w = _CMP_STRIDE * _NSA_KV
    cache5 = lambda a, b, t: a.reshape(b, t, _NSA_KVH, _NSA_DH)

    ns = Bs * Ts
    tms = min(512, ns)
    ada_s = _linear(c_sample, w_ada[l], b_ada[l])
    rowmod = lambda a, tile: jnp.broadcast_to(a[:, None, :], (Bs, Ts, _D)).reshape(ns // tile, tile, _D)
    mods = jnp.split(ada_s, 6, axis=-1)
    xs2 = x_sample.reshape(ns, _D)
    pos = past + jnp.arange(Ts, dtype=jnp.int32)
    tabs_s = _rope_tables(jnp.tile(pos, tms // Ts))
    ident = lambda i: i
    ys = _inproj(xs2, rowmod(mods[1], tms), rowmod(mods[0], tms), ident, wcat, tabs_s, lambda i: 0, tms)
    cols = lambda c, w: ys[:, c:c + w]
    assert Ts < _CMP_STRIDE
    kcb_s = _compress_paged(_pool_token_minor(cache_cmp_k[l]), page_table, cw_k).reshape(Bs, -1, _NSA_KV)
    vcb_s = _compress_paged(_pool_token_minor(cache_cmp_v[l]), page_table, cw_v).reshape(Bs, -1, _NSA_KV)
    o_ns = _nsa_sample(ys, kcb_s, vcb_s, cache_sel_k[l], cache_sel_v[l], cache_win_k[l], cache_win_v[l],
                       page_table, Bs, Ts)
    o_rs, r_new = _retention_sample(ys, state_ret[l], Bs, Ts)
    tms2 = min(256, ns)
    x1s = _merge(o_rs, ys, o_ns, xs2, rowmod(mods[2], tms2), ident, w_br_ret[l], w_br_nsa[l], w_out[l],
                 ln1_g[l], ln1_b[l], tms2)
    sel_t, main_t = min(_PEER_SEL_TOKENS, ns), min(_PEER_MAIN_TOKENS, ns)
    xs_out = _peer_and_ln(x1s, rowmod(mods[4], sel_t), rowmod(mods[3], sel_t), ident, rowmod(mods[5], main_t), ident,
                          peer_wq[l], peer_subkeys[l], u_b, v_b, ln2_g[l], ln2_b[l])
    new5 = lambda c: cache5(cols(c, _NSA_KV), Bs, Ts)
    kwin = jnp.concatenate([cache_win_k[l], new5(_C_KW)], axis=1)
    vwin = jnp.concatenate([cache_win_v[l], new5(_C_VW)], axis=1)
    sample_state = (r_new, new5(_C_KC), new5(_C_VC), new5(_C_KS), new5(_C_VS), kwin[:, Ts:], vwin[:, Ts:])

    tm = 512
    tpb = T // tm
    ada_p = _linear(c_prompt, w_ada[l], b_ada[l])
    sh1, sc1, g1, sh2, sc2, g2 = [a.reshape(B, 1, _D) for a in jnp.split(ada_p, 6, axis=-1)]
    xp2 = x_prompt.reshape(B * T, _D)
    tabs_p = _rope_tables(jnp.arange(T, dtype=jnp.int32))
    y, cache_t = _inproj(xp2, sc1, sh1, lambda i: i // tpb, wcat, tabs_p, lambda i: i % tpb, tm, cache_t_batches=B)
    col = lambda c, w: y[:, c:c + w]
    o_r, R_p = _retention_prompt(y, B, T)
    nc = T // _CMP_STRIDE
    kcb = _compress_prompt(col(_C_KC, _NSA_KV).reshape(B * nc, stride_w), B, cw_k).reshape(B, nc, _NSA_KV)
    vcb = _compress_prompt(col(_C_VC, _NSA_KV).reshape(B * nc, stride_w), B, cw_v).reshape(B, nc, _NSA_KV)
    o_n = _nsa_prompt(y, kcb, vcb, B, T)
    tm2 = 256
    tpb2 = T // tm2
    bmap2 = lambda i: i // tpb2
    x1 = _merge(o_r, y, o_n, xp2, g1, bmap2, w_br_ret[l], w_br_nsa[l], w_out[l], ln1_g[l], ln1_b[l], tm2)
    xp_out = _peer_and_ln(x1, sc2, sh2, lambda i: i // (T // _PEER_SEL_TOKENS), g2,
                          lambda i: i // (T // _PEER_MAIN_TOKENS), peer_wq[l], peer_subkeys[l], u_b, v_b,
                          ln2_g[l], ln2_b[l])
    wb = min(_WINDOW, T)
    seq5 = lambda s: jnp.transpose(cache_t[:, s].reshape(B, _NSA_KVH, _NSA_DH, T), (0, 3, 1, 2))
    prompt_state = (R_p, seq5(0), seq5(3), seq5(1), seq5(4), seq5(2)[:, T - wb:], seq5(5)[:, T - wb:])

    outs =[xp_out.reshape(B, T, _D), xs_out.reshape(Bs, Ts, _D)]
    outs += [a[None] for a in prompt_state]
    outs += [a[None] for a in sample_state]
    return tuple(outs)
```

```python
import functools
import math

import jax
import jax.numpy as jnp
import numpy as np
from jax import lax
from jax.experimental import pallas as pl
from jax.experimental.pallas import tpu as pltpu

_F32 = jnp.float32
_BF16 = jnp.bfloat16

_D = 1024
_RET_H, _RET_DK, _RET_DV, _RET_C = 4, 128, 256, 128
_NSA_H, _NSA_KVH, _NSA_G, _NSA_DH = 8, 2, 4, 64
_CMP_BLOCK, _CMP_STRIDE, _CMP_HID = 32, 16, 128
_SEL_BLOCK, _SEL_TOPN, _WINDOW, _QB = 64, 16, 512, 64
_FORCED = 1.0e4
_PEER_H, _PEER_NK, _PEER_HALF, _PEER_TOPK = 8, 128, 128, 16
_PAGE = 128
_ALPHA = 2.0 ** 0.25
_LN_EPS = 1e-5
_NEG = -1e30
_ROPE_THETA = 10000.0
_RET_QK = _RET_H * _RET_DK
_RET_V = _RET_H * _RET_DV
_NSA_Q = _NSA_H * _NSA_DH
_NSA_KV = _NSA_KVH * _NSA_DH
_IN_SECTIONS = (_RET_QK, _RET_QK, _RET_V, _RET_V, _NSA_Q, 6 * _NSA_KV, 3 * _NSA_H, 2 * _D)
_IN_SPLITS = tuple(int(s) for s in np.cumsum(_IN_SECTIONS)[:-1])
_RET_LOG_G = tuple(math.log1p(-2.0 ** (-5.0 - h)) for h in range(_RET_H))

_VMEM_LIMIT = 48 * 1024 * 1024
_INPROJ_VMEM_LIMIT = 60 * 1024 * 1024

_TN = 512
_C_RQ, _C_RK, _C_RV, _C_RG, _C_MG, _C_NQ = 0, 512, 1024, 2048, 3072, 5120
_C_KC, _C_KS, _C_KW = 5632, 5760, 5888
_C_VC, _C_VS, _C_VW, _C_NG = 6144, 6272, 6400, 6528
_IN_COLS = 13 * _TN

_SEL_CHUNK = 512
_NSA_ROWS = _NSA_KVH * _NSA_G * _QB

_NT = (((1,), (1,)), ((), ()))
_TN_DIMS = (((0,), (0,)), ((), ()))


def _cparams(sem):
    return pltpu.CompilerParams(dimension_semantics=sem, vmem_limit_bytes=_VMEM_LIMIT)


def _gelu(x):
    return 0.5 * x * (1.0 + lax.erf(x * (1.0 / math.sqrt(2.0))))


def _layer_norm(r, g, b):
    mu = jnp.mean(r, -1, keepdims=True)
    d = r - mu
    var = jnp.mean(d * d, -1, keepdims=True)
    return d * lax.rsqrt(var + _LN_EPS) * g + b


def _masked_softmax(s, mask):
    s = jnp.where(mask, s, _NEG)
    p = jnp.where(mask, jnp.exp(s - jnp.max(s, -1, keepdims=True)), 0.0)
    return p / jnp.maximum(jnp.sum(p, -1, keepdims=True), 1.0)


def _full_spec(a, grid_rank):
    zeros = (0,) * a.ndim
    return pl.BlockSpec(a.shape, {1: lambda i: zeros, 2: lambda i, j: zeros}[grid_rank])


def _linear_body(a_ref, w_ref, b_ref, o_ref):
    o_ref[...] = jnp.dot(a_ref[...].astype(_BF16), w_ref[...], preferred_element_type=_F32) + b_ref[...]


def _linear(a, w, b, tn=512):
    m, k = a.shape
    n = w.shape[1]
    return pl.pallas_call(
        _linear_body,
        grid=(n // tn,),
        in_specs=[pl.BlockSpec((m, k), lambda j: (0, 0)),
                  pl.BlockSpec((k, tn), lambda j: (0, j)),
                  pl.BlockSpec((1, tn), lambda j: (0, j))],
        out_specs=pl.BlockSpec((m, tn), lambda j: (0, j)),
        out_shape=jax.ShapeDtypeStruct((m, n), _F32),
        compiler_params=_cparams(("arbitrary",)),
        name="ada_linear",
    )(a, w.astype(_BF16), b.reshape(1, n))


def _build_wcat(w_in):
    rq, rk, rv, rg, nq, nkv, ng, mg = jnp.split(w_in, _IN_SPLITS, axis=1)
    nkv6 = nkv.reshape(_D, 6, _NSA_KV)
    kc, vc, ks, vs, kw, vw = [nkv6[:, i] for i in range(6)]
    z = jnp.zeros((_D, _NSA_KV), w_in.dtype)
    ngp = jnp.pad(ng, ((0, 0), (0, _NSA_KV - ng.shape[1])))
    wcat = jnp.concatenate([rq, rk, rv, rg, mg, nq, kc, ks, kw, z, vc, vs, vw, ngp], axis=1)
    return wcat.astype(_BF16)


def _rope_tables(pos):
    out = []
    for half in (_RET_DK // 2, _NSA_DH // 2):
        inv = _ROPE_THETA ** (-jnp.arange(half, dtype=_F32) / half)
        ang = pos.astype(_F32)[:, None] * inv[None, :]
        cos, sin = jnp.cos(ang), jnp.sin(ang)
        reps = _RET_DK // (2 * half)
        out.append(jnp.tile(jnp.concatenate([cos, cos], -1), (1, reps)))
        out.append(jnp.tile(jnp.concatenate([-sin, sin], -1), (1, reps)))
    return out


def _inproj_body(x_ref, sc_ref, sh_ref, w_ref, c128_ref, s128_ref, c64_ref, s64_ref, y_ref, yb_ref):
    a = (x_ref[...] * (1.0 + sc_ref[0]) + sh_ref[0]).astype(_BF16)
    lane = lax.broadcasted_iota(jnp.int32, (a.shape[0], _TN), 1)
    reps = _TN // _RET_DK

    def rope(y, half, c_ref, s_ref):
        first = (lane % (2 * half)) < half
        partner = jnp.where(first, pltpu.roll(y, _TN - half, 1), pltpu.roll(y, half, 1))
        return y * jnp.tile(c_ref[...], (1, reps)) + partner * jnp.tile(s_ref[...], (1, reps))

    for j in range(_IN_COLS // _TN):
        c0 = j * _TN
        y = jnp.dot(a, w_ref[:, c0:c0 + _TN], preferred_element_type=_F32)
        if c0 == _C_RQ:
            y = rope(y, _RET_DK // 2, c128_ref, s128_ref)
        elif c0 == _C_RK:
            y = rope(y, _RET_DK // 2, c128_ref, s128_ref) * (_RET_DK ** -0.5)
        elif _C_RG <= c0 < _C_MG:
            y = y * jax.nn.sigmoid(y)
        elif _C_MG <= c0 < _C_NQ:
            y = jax.nn.sigmoid(y)
        elif _C_NQ <= c0 < _C_VC:
            y = rope(y, _NSA_DH // 2, c64_ref, s64_ref)
        elif c0 == _C_VC:
            y = jnp.where(lane < _C_NG - _C_VC, y, jax.nn.sigmoid(y))
        y_ref[:, c0:c0 + _TN] = y
        if c0 >= _C_NQ:
            yb_ref[:, c0 - _C_NQ:c0 - _C_NQ + _TN] = y.astype(_BF16)


def _inproj(x2, sc3, sh3, mod_map, wcat, tabs, tab_map, tm):
    n = x2.shape[0]
    r = sc3.shape[1]
    tab_spec = pl.BlockSpec((tm, _RET_DK), lambda i: (tab_map(i), 0))
    return pl.pallas_call(
        _inproj_body,
        grid=(n // tm,),
        in_specs=[pl.BlockSpec((tm, _D), lambda i: (i, 0)),
                  pl.BlockSpec((1, r, _D), lambda i: (mod_map(i), 0, 0)),
                  pl.BlockSpec((1, r, _D), lambda i: (mod_map(i), 0, 0)),
                  pl.BlockSpec((_D, _IN_COLS), lambda i: (0, 0), pipeline_mode=pl.Buffered(1)),
                  tab_spec, tab_spec, tab_spec, tab_spec],
        out_specs=[pl.BlockSpec((tm, _IN_COLS), lambda i: (i, 0)),
                   pl.BlockSpec((tm, _IN_COLS - _C_NQ), lambda i: (i, 0))],
        out_shape=[jax.ShapeDtypeStruct((n, _IN_COLS), _F32),
                   jax.ShapeDtypeStruct((n, _IN_COLS - _C_NQ), _BF16)],
        compiler_params=pltpu.CompilerParams(dimension_semantics=("parallel",),
                                             vmem_limit_bytes=_INPROJ_VMEM_LIMIT),
        name="in_proj",
    )(x2, sc3, sh3, wcat, *tabs)


def _ret_consts(c):
    lg = jnp.asarray(_RET_LOG_G, _F32)[:, None, None]
    n = jnp.arange(c, dtype=_F32)
    diff = n[:, None] - n[None, :]
    decay = jnp.where(diff >= 0, jnp.exp(lg * jnp.maximum(diff, 0.0)), 0.0)
    cs = jnp.broadcast_to(jnp.exp(lg * (n[:, None] + 1.0)), (_RET_H, c, _RET_DV))
    kd = jnp.broadcast_to(jnp.exp(lg * (c - 1.0 - n[:, None])), (_RET_H, c, _RET_DK))
    return decay, cs, kd


def _ret_step(q_ref, k_ref, v_ref, dec_ref, cs_ref, kd_ref, o_ref, state, chunk):
    new = []
    for h in range(_RET_H):
        q = q_ref[:, h * _RET_DK:(h + 1) * _RET_DK].astype(_BF16)
        k = k_ref[:, h * _RET_DK:(h + 1) * _RET_DK]
        v = v_ref[:, h * _RET_DV:(h + 1) * _RET_DV].astype(_BF16)
        r = state(h)
        s = lax.dot_general(q, k.astype(_BF16), _NT, preferred_element_type=_F32) * dec_ref[h]
        inner = jnp.dot(s.astype(_BF16), v, preferred_element_type=_F32)
        cross = jnp.dot(q, r.astype(_BF16), preferred_element_type=_F32) * cs_ref[h]
        o_ref[:, h * _RET_DV:(h + 1) * _RET_DV] = inner + cross
        k_dec = (k * kd_ref[h]).astype(_BF16)
        new.append(r * math.exp(_RET_LOG_G[h] * chunk)
                   + lax.dot_general(k_dec, v, _TN_DIMS, preferred_element_type=_F32))
    return new


def _ret_prompt_body(q_ref, k_ref, v_ref, dec_ref, cs_ref, kd_ref, o_ref, r_ref):
    @pl.when(pl.program_id(1) == 0)
    def _():
        r_ref[...] = jnp.zeros_like(r_ref)

    new = _ret_step(q_ref, k_ref, v_ref, dec_ref, cs_ref, kd_ref, o_ref, lambda h: r_ref[0, h], _RET_C)
    for h in range(_RET_H):
        r_ref[0, h] = new[h]


def _retention_prompt(y, b, t):
    c = _RET_C
    nc = t // c
    consts = _ret_consts(c)
    return pl.pallas_call(
        _ret_prompt_body,
        grid=(b, nc),
        in_specs=[pl.BlockSpec((c, _RET_QK), lambda i, j: (i * nc + j, _C_RQ // _RET_QK)),
                  pl.BlockSpec((c, _RET_QK), lambda i, j: (i * nc + j, _C_RK // _RET_QK)),
                  pl.BlockSpec((c, _RET_V), lambda i, j: (i * nc + j, _C_RV // _RET_V))]
        + [_full_spec(a, 2) for a in consts],
        out_specs=[pl.BlockSpec((c, _RET_V), lambda i, j: (i * nc + j, 0)),
                   pl.BlockSpec((1, _RET_H, _RET_DK, _RET_DV), lambda i, j: (i, 0, 0, 0))],
        out_shape=[jax.ShapeDtypeStruct((b * t, _RET_V), _F32),
                   jax.ShapeDtypeStruct((b, _RET_H, _RET_DK, _RET_DV), _F32)],
        compiler_params=_cparams(("parallel", "arbitrary")),
        name="retention_prompt",
    )(y, y, y, *consts)


def _ret_sample_body(q_ref, k_ref, v_ref, st_ref, dec_ref, cs_ref, kd_ref, o_ref, r_ref, *, chunk):
    new = _ret_step(q_ref, k_ref, v_ref, dec_ref, cs_ref, kd_ref, o_ref, lambda h: st_ref[0, h], chunk)
    for h in range(_RET_H):
        r_ref[0, h] = new[h]


def _retention_sample(y, state, b, t):
    consts = _ret_consts(t)
    st_spec = pl.BlockSpec((1, _RET_H, _RET_DK, _RET_DV), lambda i: (i, 0, 0, 0))
    return pl.pallas_call(
        functools.partial(_ret_sample_body, chunk=t),
        grid=(b,),
        in_specs=[pl.BlockSpec((t, _RET_QK), lambda i: (i, _C_RQ // _RET_QK)),
                  pl.BlockSpec((t, _RET_QK), lambda i: (i, _C_RK // _RET_QK)),
                  pl.BlockSpec((t, _RET_V), lambda i: (i, _C_RV // _RET_V)),
                  st_spec] + [_full_spec(a, 1) for a in consts],
        out_specs=[pl.BlockSpec((t, _RET_V), lambda i: (i, 0)), st_spec],
        out_shape=[jax.ShapeDtypeStruct((b * t, _RET_V), _F32),
                   jax.ShapeDtypeStruct(state.shape, _F32)],
        compiler_params=_cparams(("parallel",)),
        name="retention_sample",
    )(y, y, y, state, *consts)


def _compress_weights(pe, w1, w2):
    w = w1.reshape(2, _CMP_STRIDE, _NSA_DH, _CMP_HID)
    eye = jnp.eye(_NSA_KVH, dtype=w1.dtype)
    wf = jnp.einsum('psde,kl->skdlpe', w, eye).reshape(_CMP_STRIDE * _NSA_KV, _NSA_KVH * 2 * _CMP_HID)
    pe8 = jnp.pad(pe.reshape(1, -1), ((0, 7), (0, 0)))
    return wf.astype(_BF16), pe8.astype(_BF16), w1.astype(_BF16), w2.astype(_BF16)


def _compress_compute(rows, wf_ref, pe_ref, w1_ref, w2_ref, o_ref, z_scr):
    ns = rows.shape[0]
    z = jnp.dot(rows.astype(_BF16), wf_ref[...], preferred_element_type=_F32)
    z_scr[pl.ds(0, ns), :] = z
    z_scr[pl.ds(ns, 8), :] = jnp.zeros((8, z.shape[1]), _F32)
    bias = jnp.dot(pe_ref[...], w1_ref[...], preferred_element_type=_F32)[0:1]
    for k in range(_NSA_KVH):
        c0 = k * 2 * _CMP_HID
        z_next = z_scr[pl.ds(1, ns), c0 + _CMP_HID:c0 + 2 * _CMP_HID]
        hid = _gelu(z[:, c0:c0 + _CMP_HID] + z_next + bias)
        o_ref[:, k * _NSA_DH:(k + 1) * _NSA_DH] = jnp.dot(hid.astype(_BF16), w2_ref[...],
                                                         preferred_element_type=_F32)


def _compress_body(rows_ref, wf_ref, pe_ref, w1_ref, w2_ref, o_ref, z_scr):
    _compress_compute(rows_ref[...], wf_ref, pe_ref, w1_ref, w2_ref, o_ref, z_scr)


def _compress_prompt(rows2, b, weights):
    ns = rows2.shape[0] // b
    return pl.pallas_call(
        _compress_body,
        grid=(b,),
        in_specs=[pl.BlockSpec((ns, rows2.shape[1]), lambda i: (i, 0))] + [_full_spec(a, 1) for a in weights],
        out_specs=pl.BlockSpec((ns, _NSA_KV), lambda i: (i, 0)),
        out_shape=jax.ShapeDtypeStruct((b * ns, _NSA_KV), _F32),
        scratch_shapes=[pltpu.VMEM((ns + 8, _NSA_KVH * 2 * _CMP_HID), _F32)],
        compiler_params=_cparams(("parallel",)),
        name="compress_prompt",
    )(rows2, *weights)


def _pool_token_minor(pool):
    return jnp.transpose(pool, (0, 2, 3, 1))


def _page_copies(pt_ref, b, pool_ref, dst_ref, sem):
    n_pages = pt_ref.shape[1]
    return [pltpu.make_async_copy(pool_ref.at[pt_ref[b, p]], dst_ref.at[:, :, pl.ds(p * _PAGE, _PAGE)], sem)
            for p in range(n_pages)]


def _prefetched_pages(copies_of):
    b = pl.program_id(0)
    slot = b % 2

    @pl.when(b == 0)
    def _():
        for cp in copies_of(b, slot):
            cp.start()

    @pl.when(b + 1 < pl.num_programs(0))
    def _():
        for cp in copies_of(b + 1, 1 - slot):
            cp.start()

    for cp in copies_of(b, slot):
        cp.wait()
    return slot


def _compress_paged_body(pt_ref, pool_ref, wf_ref, pe_ref, w1_ref, w2_ref, o_ref, xt_scr, x_scr, rows_scr, z_scr,
                         sem):
    slot = _prefetched_pages(lambda b, s: _page_copies(pt_ref, b, pool_ref, xt_scr.at[s], sem.at[s]))
    past = xt_scr.shape[3]
    ns = past // _CMP_STRIDE
    x_scr[...] = xt_scr[slot].reshape(_NSA_KV, past).T
    for s in range(_CMP_STRIDE):
        rows_scr[:, s * _NSA_KV:(s + 1) * _NSA_KV] = x_scr[pl.ds(s, ns, stride=_CMP_STRIDE), :]
    _compress_compute(rows_scr[...], wf_ref, pe_ref, w1_ref, w2_ref, o_ref, z_scr)


def _compress_paged(pool_t, page_table, weights):
    b, n_pages = page_table.shape
    past = n_pages * _PAGE
    ns = past // _CMP_STRIDE
    grid_spec = pltpu.PrefetchScalarGridSpec(
        num_scalar_prefetch=1,
        grid=(b,),
        in_specs=[pl.BlockSpec(memory_space=pl.ANY)]
        + [pl.BlockSpec(a.shape, lambda i, pt, nd=a.ndim: (0,) * nd) for a in weights],
        out_specs=pl.BlockSpec((ns, _NSA_KV), lambda i, pt: (i, 0)),
        scratch_shapes=[pltpu.VMEM((2, _NSA_KVH, _NSA_DH, past), _F32),
                        pltpu.VMEM((past, _NSA_KV), _F32),
                        pltpu.VMEM((ns, _CMP_STRIDE * _NSA_KV), _F32),
                        pltpu.VMEM((ns + 8, _NSA_KVH * 2 * _CMP_HID), _F32),
                        pltpu.SemaphoreType.DMA((2,))],
    )
    return pl.pallas_call(
        _compress_paged_body,
        grid_spec=grid_spec,
        out_shape=jax.ShapeDtypeStruct((b * ns, _NSA_KV), _F32),
        compiler_params=_cparams(("arbitrary",)),
        name="compress_paged",
    )(page_table, pool_t, *weights)


def _overlap_matrix(n_cmp, n_sel_pad):
    nn = lax.broadcasted_iota(jnp.int32, (n_cmp, n_sel_pad), 0)
    jj = lax.broadcasted_iota(jnp.int32, (n_cmp, n_sel_pad), 1)
    per = _SEL_BLOCK // _CMP_STRIDE
    return ((nn < (jj + 1) * per) & (nn * _CMP_STRIDE + _CMP_BLOCK > jj * _SEL_BLOCK)).astype(_BF16)


def _importance(pcs, ov):
    hi = pcs.astype(_BF16)
    lo = (pcs - hi.astype(_F32)).astype(_BF16)
    return jnp.dot(hi, ov, preferred_element_type=_F32) + jnp.dot(lo, ov, preferred_element_type=_F32)


def _topk_mask_rows(work, k):
    r = work.shape[0]
    iota = lax.broadcasted_iota(jnp.int32, work.shape, 0)
    sel = jnp.zeros(work.shape, _F32)
    for _ in range(k):
        mx = jnp.max(work, axis=0, keepdims=True)
        idx = jnp.min(jnp.where(work == mx, iota, r), axis=0, keepdims=True)
        pick = iota == idx
        sel = jnp.where(pick, 1.0, sel)
        work = jnp.where(pick, -jnp.inf, work)
    return sel


def _nsa_prompt_body(q_ref, g_ref, kc_ref, vc_ref, ks_ref, vs_ref, kw_ref, vw_ref, o_ref, *, seq):
    qb = pl.program_id(1)
    rows = _NSA_ROWS
    n_sel_pad = ks_ref.shape[3] - _NSA_KV
    q = q_ref[0, 0]
    row = lax.broadcasted_iota(jnp.int32, (rows, 1), 0)
    tpos = qb * _QB + row % _QB

    n_cmp = kc_ref.shape[1]
    s_c = lax.dot_general(q, kc_ref[0], _NT, preferred_element_type=_F32)
    n_i = lax.broadcasted_iota(jnp.int32, (1, n_cmp), 1)
    p_c = _masked_softmax(s_c, (n_i * _CMP_STRIDE + _CMP_BLOCK - 1) <= tpos)
    o_c = jnp.dot(p_c.astype(_BF16), vc_ref[0], preferred_element_type=_F32)

    gq = _NSA_G * _QB
    pcs = jnp.concatenate(
        [functools.reduce(lambda u, v: u + v, [p_c[k * gq + g * _QB:k * gq + (g + 1) * _QB] for g in range(_NSA_G)])
         for k in range(_NSA_KVH)], axis=0)
    imp = _importance(pcs, _overlap_matrix(n_cmp, n_sel_pad))
    j_i = lax.broadcasted_iota(jnp.int32, imp.shape, 1)
    forced = (j_i == 0) | (j_i == qb) | (j_i == qb - 1)
    score = jnp.where(j_i <= qb, jnp.where(forced, _FORCED, imp), -1.0)
    score_t = score.T
    sel = (_topk_mask_rows(score_t, _SEL_TOPN) * (score_t >= 0.0).astype(_F32)).T
    bias = jnp.where(sel > 0.5, 0.0, _NEG).astype(_BF16)
    bias = jnp.concatenate([bias[k * _QB:(k + 1) * _QB] for k in range(_NSA_KVH) for _ in range(_NSA_G)], axis=0)

    q_sel = jnp.concatenate([q, bias], axis=1)
    k_i = lax.broadcasted_iota(jnp.int32, (1, _SEL_CHUNK), 1)

    def chunk_step(c, carry, causal):
        m, l, acc = carry
        s = lax.dot_general(q_sel, ks_ref[0, c], _NT, preferred_element_type=_F32)
        if causal:
            s = jnp.where((c * _SEL_CHUNK + k_i) <= tpos, s, _NEG)
        m_new = jnp.maximum(m, jnp.max(s, -1, keepdims=True))
        p = jnp.exp(s - m_new)
        alpha = jnp.exp(m - m_new)
        l = alpha * l + jnp.sum(p, -1, keepdims=True)
        acc = alpha * acc + jnp.dot(p.astype(_BF16), vs_ref[0, c], preferred_element_type=_F32)
        return m_new, l, acc

    init = (jnp.full((rows, 1), _NEG, _F32), jnp.zeros((rows, 1), _F32), jnp.zeros((rows, _NSA_KV), _F32))
    last = qb // (_SEL_CHUNK // _SEL_BLOCK)
    carry = lax.fori_loop(0, last, lambda c, cr: chunk_step(c, cr, False), init)
    _, l_s, acc_s = chunk_step(last, carry, True)
    o_s = acc_s / jnp.maximum(l_s, 1.0)

    wlen = _WINDOW + 2 * _QB
    w0 = pl.multiple_of(jnp.clip(qb * _QB - (_WINDOW + _QB), 0, seq - wlen), _QB)
    s_w = lax.dot_general(q, kw_ref[0, pl.ds(w0, wlen), :], _NT, preferred_element_type=_F32)
    kpos = w0 + lax.broadcasted_iota(jnp.int32, (1, wlen), 1)
    p_w = _masked_softmax(s_w, (kpos <= tpos) & (kpos > tpos - _WINDOW))
    o_w = jnp.dot(p_w.astype(_BF16), vw_ref[0, pl.ds(w0, wlen), :], preferred_element_type=_F32)

    g = g_ref[0, 0]
    o_ref[0, 0] = g[:, 0:1] * o_c + g[:, 1:2] * o_s + g[:, 2:3] * o_w


def _block_diag_queries(nq):
    *lead, t, _, _ = nq.shape
    q = (nq * (_NSA_DH ** -0.5)).astype(_BF16).reshape(*lead, t, _NSA_KVH, _NSA_G, _NSA_DH)
    q = jnp.moveaxis(q, -4, -2)
    eye = jnp.eye(_NSA_KVH, dtype=_BF16)
    q = q[..., None, :] * eye[:, None, None, :, None]
    return q.reshape(*lead, _NSA_KVH * _NSA_G * t, _NSA_KV)


def _gate_rows(ng):
    *lead, t, _, _, _ = ng.shape
    return jnp.moveaxis(ng, -4, -2).reshape(*lead, _NSA_KVH * _NSA_G * t, 3)


def _own_head_rows(o, t):
    lead = o.shape[:-2]
    o = o.reshape(*lead, _NSA_KVH, _NSA_G, t, _NSA_KVH, _NSA_DH)
    o = jnp.stack([o[..., k, :, :, k, :] for k in range(_NSA_KVH)], axis=-4)
    return jnp.moveaxis(o, -2, -4).reshape(*lead, t, _NSA_Q)


def _nsa_prompt(y, yb, kcb, vcb, b, t):
    nqb = t // _QB
    n_sel_pad = 128
    assert nqb <= n_sel_pad and t % _SEL_CHUNK == 0
    col = lambda c, w: y[:, c:c + w]
    colb = lambda c, w: yb[:, c - _C_NQ:c - _C_NQ + w]
    q = _block_diag_queries(colb(_C_NQ, _NSA_Q).reshape(b, nqb, _QB, _NSA_H, _NSA_DH))
    g = _gate_rows(col(_C_NG, 3 * _NSA_H).reshape(b, nqb, _QB, _NSA_KVH, _NSA_G, 3))
    seq_b = lambda c: colb(c, _NSA_KV).reshape(b, t, _NSA_KV)
    chunked = lambda a: a.reshape(b, t // _SEL_CHUNK, _SEL_CHUNK, a.shape[-1])
    block_id = (jnp.arange(t)[:, None] // _SEL_BLOCK == jnp.arange(n_sel_pad)[None, :]).astype(_BF16)
    ks = chunked(jnp.concatenate([seq_b(_C_KS), jnp.broadcast_to(block_id, (b, t, n_sel_pad))], axis=-1))
    vs, kw, vw = chunked(seq_b(_C_VS)), seq_b(_C_KW), seq_b(_C_VW)
    kc, vc = kcb.astype(_BF16), vcb.astype(_BF16)
    per_q = lambda last: pl.BlockSpec((1, 1, _NSA_ROWS, last), lambda i, j: (i, j, 0, 0))
    per_b = lambda a: pl.BlockSpec((1,) + a.shape[1:], lambda i, j: (i,) + (0,) * (a.ndim - 1))
    o = pl.pallas_call(
        functools.partial(_nsa_prompt_body, seq=t),
        grid=(b, nqb),
        in_specs=[per_q(_NSA_KV), per_q(3), per_b(kc), per_b(vc), per_b(ks), per_b(vs), per_b(kw), per_b(vw)],
        out_specs=per_q(_NSA_KV),
        out_shape=jax.ShapeDtypeStruct((b, nqb, _NSA_ROWS, _NSA_KV), _F32),
        compiler_params=_cparams(("parallel", "arbitrary")),
        name="nsa_prompt",
    )(q, g, kc, vc, ks, vs, kw, vw)
    return _own_head_rows(o, _QB).reshape(b * t, _NSA_Q)


def _nsa_sample_body(pt_ref, q_ref, g_ref, kc_ref, vc_ref, ksn_ref, vsn_ref, kwc_ref, vwc_ref, kwn_ref, vwn_ref,
                     e_ref, poolk_ref, poolv_ref, o_ref, ksel_scr, vsel_scr, sem, *, past, ts):
    slot = _prefetched_pages(lambda b, s: (_page_copies(pt_ref, b, poolk_ref, ksel_scr.at[s], sem.at[0, s])
                                           + _page_copies(pt_ref, b, poolv_ref, vsel_scr.at[s], sem.at[1, s])))

    rows = _NSA_KVH * _NSA_G * ts
    q = q_ref[0]
    row = lax.broadcasted_iota(jnp.int32, (rows, 1), 0)
    t_new = row % ts
    tpos = past + t_new
    new_ok = lax.broadcasted_iota(jnp.int32, (1, ts), 1) <= t_new

    n_cmp = kc_ref.shape[1]
    s_c = lax.dot_general(q, kc_ref[0], _NT, preferred_element_type=_F32)
    n_i = lax.broadcasted_iota(jnp.int32, (1, n_cmp), 1)
    p_c = _masked_softmax(s_c, (n_i * _CMP_STRIDE + _CMP_BLOCK - 1) <= tpos)
    o_c = jnp.dot(p_c.astype(_BF16), vc_ref[0], preferred_element_type=_F32)

    n_past = e_ref.shape[0]
    gt = _NSA_G * ts
    pcs = jnp.concatenate(
        [functools.reduce(lambda u, v: u + v, [p_c[k * gt + g * ts:k * gt + (g + 1) * ts] for g in range(_NSA_G)])
         for k in range(_NSA_KVH)], axis=0)
    imp = _importance(pcs, _overlap_matrix(n_cmp, n_past))
    j_i = lax.broadcasted_iota(jnp.int32, imp.shape, 1)
    cur = past // _SEL_BLOCK
    score = jnp.where((j_i == 0) | (j_i == cur - 1), _FORCED, imp)
    sel = _topk_mask_rows(score.T, _SEL_TOPN - 1).T
    bias = jnp.where(sel > 0.5, 0.0, _NEG).astype(_BF16)
    bias = jnp.concatenate([bias[k * ts:(k + 1) * ts] for k in range(_NSA_KVH) for _ in range(_NSA_G)], axis=0)

    wb = kwc_ref.shape[2]
    s_w = jnp.dot(q, kwc_ref[0].astype(_BF16), preferred_element_type=_F32)
    w_i = lax.broadcasted_iota(jnp.int32, (1, wb), 1)
    w_ok = (past - wb + w_i) > (tpos - _WINDOW)
    s_w = jnp.where(w_ok, s_w, _NEG)
    s_wn = jnp.where(new_ok, lax.dot_general(q, kwn_ref[0], _NT, preferred_element_type=_F32), _NEG)
    m_w = jnp.maximum(jnp.max(s_w, -1, keepdims=True), jnp.max(s_wn, -1, keepdims=True))
    p_w = jnp.where(w_ok, jnp.exp(s_w - m_w), 0.0)
    p_wn = jnp.where(new_ok, jnp.exp(s_wn - m_w), 0.0)
    l_w = jnp.sum(p_w, -1, keepdims=True) + jnp.sum(p_wn, -1, keepdims=True)
    o_w = (lax.dot_general(p_w.astype(_BF16), vwc_ref[0].astype(_BF16), _NT, preferred_element_type=_F32)
           + jnp.dot(p_wn.astype(_BF16), vwn_ref[0], preferred_element_type=_F32)) / jnp.maximum(l_w, 1.0)

    k_sel = ksel_scr[slot].reshape(_NSA_KV, past).astype(_BF16)
    v_sel = vsel_scr[slot].reshape(_NSA_KV, past).astype(_BF16)
    s_s = (jnp.dot(q, k_sel, preferred_element_type=_F32)
           + jnp.dot(bias, e_ref[...], preferred_element_type=_F32))
    s_sn = jnp.where(new_ok, lax.dot_general(q, ksn_ref[0], _NT, preferred_element_type=_F32), _NEG)
    m_s = jnp.maximum(jnp.max(s_s, -1, keepdims=True), jnp.max(s_sn, -1, keepdims=True))
    p_s = jnp.exp(s_s - m_s)
    p_sn = jnp.where(new_ok, jnp.exp(s_sn - m_s), 0.0)
    l_s = jnp.sum(p_s, -1, keepdims=True) + jnp.sum(p_sn, -1, keepdims=True)
    o_s = (lax.dot_general(p_s.astype(_BF16), v_sel, _NT, preferred_element_type=_F32)
           + jnp.dot(p_sn.astype(_BF16), vsn_ref[0], preferred_element_type=_F32)) / jnp.maximum(l_s, 1.0)

    g = g_ref[0]
    o_ref[0] = g[:, 0:1] * o_c + g[:, 1:2] * o_s + g[:, 2:3] * o_w


def _nsa_sample(y, kcb, vcb, pool_k, pool_v, win_k, win_v, page_table, b, ts):
    n_pages = page_table.shape[1]
    past = n_pages * _PAGE
    n_past = past // _SEL_BLOCK
    assert ts <= _SEL_BLOCK and _SEL_TOPN - 1 <= n_past <= 128
    rows = _NSA_KVH * _NSA_G * ts
    col = lambda c, w: y[:, c:c + w]
    q = _block_diag_queries(col(_C_NQ, _NSA_Q).reshape(b, ts, _NSA_H, _NSA_DH))
    g = _gate_rows(col(_C_NG, 3 * _NSA_H).reshape(b, ts, _NSA_KVH, _NSA_G, 3))
    new = lambda c: col(c, _NSA_KV).astype(_BF16).reshape(b, ts, _NSA_KV)
    expand = (jnp.arange(past)[None, :] // _SEL_BLOCK == jnp.arange(n_past)[:, None]).astype(_BF16)
    per_b = lambda a: pl.BlockSpec((1,) + a.shape[1:], lambda i, pt, nd=a.ndim: (i,) + (0,) * (nd - 1))
    win_t = lambda a: _pool_token_minor(a).reshape(b, _NSA_KV, a.shape[1])
    blocked = [q, g, kcb.astype(_BF16), vcb.astype(_BF16), new(_C_KS), new(_C_VS),
               win_t(win_k), win_t(win_v), new(_C_KW), new(_C_VW)]
    pools = [_pool_token_minor(pool_k), _pool_token_minor(pool_v)]
    grid_spec = pltpu.PrefetchScalarGridSpec(
        num_scalar_prefetch=1,
        grid=(b,),
        in_specs=[per_b(a) for a in blocked] + [pl.BlockSpec(expand.shape, lambda i, pt: (0, 0))]
        + [pl.BlockSpec(memory_space=pl.ANY)] * 2,
        out_specs=pl.BlockSpec((1, rows, _NSA_KV), lambda i, pt: (i, 0, 0)),
        scratch_shapes=[pltpu.VMEM((2, _NSA_KVH, _NSA_DH, past), _F32),
                        pltpu.VMEM((2, _NSA_KVH, _NSA_DH, past), _F32),
                        pltpu.SemaphoreType.DMA((2, 2))],
    )
    o = pl.pallas_call(
        functools.partial(_nsa_sample_body, past=past, ts=ts),
        grid_spec=grid_spec,
        out_shape=jax.ShapeDtypeStruct((b, rows, _NSA_KV), _F32),
        compiler_params=_cparams(("arbitrary",)),
        name="nsa_sample",
    )(page_table, *blocked, expand, *pools)
    return _own_head_rows(o, ts).reshape(b * ts, _NSA_Q)


def _merge_body(o_ref, srg_ref, on_ref, mg0_ref, mg1_ref, x_ref, g1_ref, wbr_ref, wbn_ref, wout_ref,
                lng_ref, lnb_ref, out_ref):
    o = o_ref[...]
    parts = []
    for h in range(_RET_H):
        seg = o[:, h * _RET_DV:(h + 1) * _RET_DV]
        mu = jnp.mean(seg, -1, keepdims=True)
        d = seg - mu
        var = jnp.mean(d * d, -1, keepdims=True)
        parts.append(d * lax.rsqrt(var + _LN_EPS))
    a = (jnp.concatenate(parts, axis=1) * srg_ref[...]).astype(_BF16)
    br = jnp.dot(a, wbr_ref[...], preferred_element_type=_F32)
    bn = jnp.dot(on_ref[...].astype(_BF16), wbn_ref[...], preferred_element_type=_F32)
    m = (mg0_ref[...] * br + mg1_ref[...] * bn).astype(_BF16)
    yv = jnp.dot(m, wout_ref[...], preferred_element_type=_F32)
    r = _ALPHA * x_ref[...] + g1_ref[0] * yv
    out_ref[...] = _layer_norm(r, lng_ref[...], lnb_ref[...])


def _merge(o_r, y, o_n, x2, g13, mod_map, w_br_ret, w_br_nsa, w_out, ln_g, ln_b, tm):
    n = x2.shape[0]
    r = g13.shape[1]
    row = lambda c: pl.BlockSpec((tm, _D), lambda i: (i, c))
    wbr, wbn, wout = w_br_ret.astype(_BF16), w_br_nsa.astype(_BF16), w_out.astype(_BF16)
    lng, lnb = ln_g.reshape(1, _D), ln_b.reshape(1, _D)
    return pl.pallas_call(
        _merge_body,
        grid=(n // tm,),
        in_specs=[row(0), row(_C_RG // _D), pl.BlockSpec((tm, _NSA_Q), lambda i: (i, 0)),
                  row(_C_MG // _D), row(_C_MG // _D + 1), row(0),
                  pl.BlockSpec((1, r, _D), lambda i: (mod_map(i), 0, 0))]
        + [_full_spec(a, 1) for a in (wbr, wbn, wout, lng, lnb)],
        out_specs=row(0),
        out_shape=jax.ShapeDtypeStruct((n, _D), _F32),
        compiler_params=_cparams(("parallel",)),
        name="merge_ln1",
    )(o_r, y, o_n, y, y, x2, g13, wbr, wbn, wout, lng, lnb)


def _topk_rows(work, k):
    r = work.shape[0]
    iota = lax.broadcasted_iota(jnp.int32, work.shape, 0)
    vals = []
    for _ in range(k):
        mx = jnp.max(work, axis=0, keepdims=True)
        idx = jnp.min(jnp.where(work == mx, iota, r), axis=0, keepdims=True)
        work = jnp.where(iota == idx, -jnp.inf, work)
        vals.append(mx)
    return vals


def _peer_select_body(x_ref, sc_ref, sh_ref, wq_ref, sk_ref, xb_ref, st_ref, thr_ref, mz_ref, sv_scr):
    xb = (x_ref[...] * (1.0 + sc_ref[0]) + sh_ref[0]).astype(_BF16)
    xb_ref[...] = xb
    q = jnp.dot(xb, wq_ref[...], preferred_element_type=_F32).astype(_BF16)
    k = _PEER_TOPK
    neg = jnp.full((1, q.shape[0]), -jnp.inf, _F32)
    for h in range(_PEER_H):
        for p in range(2):
            hp = 2 * h + p
            st = lax.dot_general(sk_ref[hp], q[:, hp * _PEER_HALF:(hp + 1) * _PEER_HALF], _NT,
                                 preferred_element_type=_F32)
            st_ref[hp] = st
            for r, v in enumerate(_topk_rows(st, k)):
                sv_scr[p, r:r + 1, :] = v
        sv0, sv1 = sv_scr[0], sv_scr[1]
        row8 = lax.broadcasted_iota(jnp.int32, (8, 1), 0)
        row16 = lax.broadcasted_iota(jnp.int32, (k, 1), 0)
        groups = [sv0[0:1] + sv1, jnp.where(row16 == 0, neg, sv0 + sv1[0:1])]
        for a in range(1, 8):
            groups.append(jnp.where(row8 == 0, neg, sv0[a:a + 1] + sv1[0:8]))
        top = _topk_rows(jnp.concatenate(groups, axis=0), k)
        m = top[0]
        z = functools.reduce(lambda u, v: u + v, [jnp.exp(t - m) for t in top])
        thr_ref[h:h + 1, :] = top[k - 1]
        mz_ref[h:h + 1, :] = m + jnp.log(z)


def _peer_select(x1, sc3, sh3, mod_map, wq, subkeys, tn):
    n = x1.shape[0]
    r = sc3.shape[1]
    hp = 2 * _PEER_H
    wqb = wq.astype(_BF16)
    sk = subkeys.astype(_BF16).reshape(hp, _PEER_NK, _PEER_HALF)
    mod = pl.BlockSpec((1, r, _D), lambda i: (mod_map(i), 0, 0))
    return pl.pallas_call(
        _peer_select_body,
        grid=(n // tn,),
        in_specs=[pl.BlockSpec((tn, _D), lambda i: (i, 0)), mod, mod, _full_spec(wqb, 1), _full_spec(sk, 1)],
        out_specs=[pl.BlockSpec((tn, _D), lambda i: (i, 0)),
                   pl.BlockSpec((hp, _PEER_NK, tn), lambda i: (0, 0, i)),
                   pl.BlockSpec((_PEER_H, tn), lambda i: (0, i)),
                   pl.BlockSpec((_PEER_H, tn), lambda i: (0, i))],
        out_shape=[jax.ShapeDtypeStruct((n, _D), _BF16),
                   jax.ShapeDtypeStruct((hp, _PEER_NK, n), _F32),
                   jax.ShapeDtypeStruct((_PEER_H, n), _F32),
                   jax.ShapeDtypeStruct((_PEER_H, n), _F32)],
        scratch_shapes=[pltpu.VMEM((2, _PEER_TOPK, tn), _F32)],
        compiler_params=_cparams(("parallel",)),
        name="peer_select",
    )(x1, sc3, sh3, wqb, sk)


def _peer_main_body(xb_ref, u_ref, v_ref, st_ref, thr_ref, mz_ref, x_ref, g2_ref, lng_ref, lnb_ref,
                    out_ref, acc_ref, e_scr, g_scr):
    e = pl.program_id(1)
    slabs = u_ref.shape[0] // _PEER_NK
    n = xb_ref.shape[0]

    @pl.when(e == 0)
    def _():
        acc_ref[...] = jnp.zeros_like(acc_ref)
        for h in range(_PEER_H):
            s1 = st_ref[2 * h + 1]
            m1 = jnp.max(s1, axis=0, keepdims=True)
            e_scr[2 * h] = jnp.exp(st_ref[2 * h] - (mz_ref[h:h + 1, :] - m1))
            e_scr[2 * h + 1] = jnp.exp(s1 - m1)

    for sl in range(slabs):
        i = e * slabs + sl
        gt = jnp.zeros((_PEER_NK, n), _F32)
        for h in range(_PEER_H):
            c = st_ref[2 * h, pl.ds(i, 1), :] + st_ref[2 * h + 1]
            gt = jnp.where(c >= thr_ref[h:h + 1, :], gt + e_scr[2 * h, pl.ds(i, 1), :] * e_scr[2 * h + 1], gt)
        g_scr[sl * _PEER_NK:(sl + 1) * _PEER_NK, :] = gt
    act = _gelu(lax.dot_general(u_ref[...], xb_ref[...], _NT, preferred_element_type=_F32))
    wt = (act * g_scr[...]).astype(_BF16)
    acc_ref[...] += lax.dot_general(wt, v_ref[...], _TN_DIMS, preferred_element_type=_F32)

    @pl.when(e == pl.num_programs(1) - 1)
    def _():
        r = _ALPHA * x_ref[...] + g2_ref[0] * acc_ref[...]
        out_ref[...] = _layer_norm(r, lng_ref[...], lnb_ref[...])


def _peer_main(xb, u_b, v_b, st, thr, mz, x1, g23, mod_map, ln_g, ln_b, tn, ec):
    n = x1.shape[0]
    r = g23.shape[1]
    ne = u_b.shape[0]
    hp = 2 * _PEER_H
    lng, lnb = ln_g.reshape(1, _D), ln_b.reshape(1, _D)
    return pl.pallas_call(
        _peer_main_body,
        grid=(n // tn, ne // ec),
        in_specs=[pl.BlockSpec((tn, _D), lambda i, e: (i, 0)),
                  pl.BlockSpec((ec, _D), lambda i, e: (e, 0)),
                  pl.BlockSpec((ec, _D), lambda i, e: (e, 0)),
                  pl.BlockSpec((hp, _PEER_NK, tn), lambda i, e: (0, 0, i)),
                  pl.BlockSpec((_PEER_H, tn), lambda i, e: (0, i)),
                  pl.BlockSpec((_PEER_H, tn), lambda i, e: (0, i)),
                  pl.BlockSpec((tn, _D), lambda i, e: (i, 0)),
                  pl.BlockSpec((1, r, _D), lambda i, e: (mod_map(i), 0, 0)),
                  _full_spec(lng, 2), _full_spec(lnb, 2)],
        out_specs=pl.BlockSpec((tn, _D), lambda i, e: (i, 0)),
        out_shape=jax.ShapeDtypeStruct((n, _D), _F32),
        scratch_shapes=[pltpu.VMEM((tn, _D), _F32), pltpu.VMEM((hp, _PEER_NK, tn), _F32),
                        pltpu.VMEM((ec, tn), _F32)],
        compiler_params=_cparams(("parallel", "arbitrary")),
        name="peer_main",
    )(xb, u_b, v_b, st, thr, mz, x1, g23, lng, lnb)


_PEER_SEL_TOKENS = 256
_PEER_MAIN_TOKENS = 512
_PEER_EXPERT_CHUNK = 1024


def _peer_and_ln(x1, sc3, sh3, sel_map, g3, main_map, wq, subkeys, u_b, v_b, ln_g, ln_b):
    n = x1.shape[0]
    xb, st, thr, mz = _peer_select(x1, sc3, sh3, sel_map, wq, subkeys, min(_PEER_SEL_TOKENS, n))
    return _peer_main(xb, u_b, v_b, st, thr, mz, x1, g3, main_map, ln_g, ln_b, min(_PEER_MAIN_TOKENS, n),
                      _PEER_EXPERT_CHUNK)


def kernel(x_prompt, x_sample, c_prompt, c_sample, state_ret, cache_cmp_k, cache_cmp_v, cache_sel_k, cache_sel_v,
           cache_win_k, cache_win_v, page_table, w_ada, b_ada, w_in, cmp_pe_k, cmp_w1_k, cmp_w2_k, cmp_pe_v,
           cmp_w1_v, cmp_w2_v, w_br_ret, w_br_nsa, w_out, ln1_g, ln1_b, peer_wq, peer_subkeys, peer_u, peer_v,
           ln2_g, ln2_b):
    l = 0
    B, T, _ = x_prompt.shape
    Bs, Ts, _ = x_sample.shape
    n_pages = page_table.shape[1]
    past = n_pages * _PAGE
    wcat = _build_wcat(w_in[l])
    u_b, v_b = peer_u[l].astype(_BF16), peer_v[l].astype(_BF16)
    cw_k = _compress_weights(cmp_pe_k[l], cmp_w1_k[l], cmp_w2_k[l])
    cw_v = _compress_weights(cmp_pe_v[l], cmp_w1_v[l], cmp_w2_v[l])
    stride_w = _CMP_STRIDE * _NSA_KV
    cache5 = lambda a, b, t: a.reshape(b, t, _NSA_KVH, _NSA_DH)

    ns = Bs * Ts
    tms = min(512, ns)
    ada_s = _linear(c_sample, w_ada[l], b_ada[l])
    rowmod = lambda a, tile: jnp.broadcast_to(a[:, None, :], (Bs, Ts, _D)).reshape(ns // tile, tile, _D)
    mods = jnp.split(ada_s, 6, axis=-1)
    xs2 = x_sample.reshape(ns, _D)
    pos = past + jnp.arange(Ts, dtype=jnp.int32)
    tabs_s = _rope_tables(jnp.tile(pos, tms // Ts))
    ident = lambda i: i
    ys, _ = _inproj(xs2, rowmod(mods[1], tms), rowmod(mods[0], tms), ident, wcat, tabs_s, lambda i: 0, tms)
    cols = lambda c, w: ys[:, c:c + w]
    assert Ts < _CMP_STRIDE
    kcb_s = _compress_paged(_pool_token_minor(cache_cmp_k[l]), page_table, cw_k).reshape(Bs, -1, _NSA_KV)
    vcb_s = _compress_paged(_pool_token_minor(cache_cmp_v[l]), page_table, cw_v).reshape(Bs, -1, _NSA_KV)
    o_ns = _nsa_sample(ys, kcb_s, vcb_s, cache_sel_k[l], cache_sel_v[l], cache_win_k[l], cache_win_v[l],
                       page_table, Bs, Ts)
    o_rs, r_new = _retention_sample(ys, state_ret[l], Bs, Ts)
    tms2 = min(256, ns)
    x1s = _merge(o_rs, ys, o_ns, xs2, rowmod(mods[2], tms2), ident, w_br_ret[l], w_br_nsa[l], w_out[l],
                 ln1_g[l], ln1_b[l], tms2)
    sel_t, main_t = min(_PEER_SEL_TOKENS, ns), min(_PEER_MAIN_TOKENS, ns)
    xs_out = _peer_and_ln(x1s, rowmod(mods[4], sel_t), rowmod(mods[3], sel_t), ident, rowmod(mods[5], main_t), ident,
                          peer_wq[l], peer_subkeys[l], u_b, v_b, ln2_g[l], ln2_b[l])
    new5 = lambda c: cache5(cols(c, _NSA_KV), Bs, Ts)
    kwin = jnp.concatenate([cache_win_k[l], new5(_C_KW)], axis=1)
    vwin = jnp.concatenate([cache_win_v[l], new5(_C_VW)], axis=1)
    sample_state = (r_new, new5(_C_KC), new5(_C_VC), new5(_C_KS), new5(_C_VS), kwin[:, Ts:], vwin[:, Ts:])

    tm = 512
    tpb = T // tm
    ada_p = _linear(c_prompt, w_ada[l], b_ada[l])
    sh1, sc1, g1, sh2, sc2, g2 = [a.reshape(B, 1, _D) for a in jnp.split(ada_p, 6, axis=-1)]
    xp2 = x_prompt.reshape(B * T, _D)
    tabs_p = _rope_tables(jnp.arange(T, dtype=jnp.int32))
    y, yb = _inproj(xp2, sc1, sh1, lambda i: i // tpb, wcat, tabs_p, lambda i: i % tpb, tm)
    col = lambda c, w: y[:, c:c + w]
    o_r, R_p = _retention_prompt(y, B, T)
    nc = T // _CMP_STRIDE
    kcb = _compress_prompt(col(_C_KC, _NSA_KV).reshape(B * nc, stride_w), B, cw_k).reshape(B, nc, _NSA_KV)
    vcb = _compress_prompt(col(_C_VC, _NSA_KV).reshape(B * nc, stride_w), B, cw_v).reshape(B, nc, _NSA_KV)
    o_n = _nsa_prompt(y, yb, kcb, vcb, B, T)
    tm2 = 256
    tpb2 = T // tm2
    bmap2 = lambda i: i // tpb2
    x1 = _merge(o_r, y, o_n, xp2, g1, bmap2, w_br_ret[l], w_br_nsa[l], w_out[l], ln1_g[l], ln1_b[l], tm2)
    xp_out = _peer_and_ln(x1, sc2, sh2, lambda i: i // (T // _PEER_SEL_TOKENS), g2,
                          lambda i: i // (T // _PEER_MAIN_TOKENS), peer_wq[l], peer_subkeys[l], u_b, v_b,
                          ln2_g[l], ln2_b[l])
    wb = min(_WINDOW, T)
    seq5 = lambda c: cache5(col(c, _NSA_KV), B, T)
    prompt_state = (R_p, seq5(_C_KC), seq5(_C_VC), seq5(_C_KS), seq5(_C_VS),
                    seq5(_C_KW)[:, T - wb:], seq5(_C_VW)[:, T - wb:])

    outs =[xp_out.reshape(B, T, _D), xs_out.reshape(Bs, Ts, _D)]
    outs += [a[None] for a in prompt_state]
    outs += [a[None] for a in sample_state]
    return tuple(outs)
```
